```python
import math
import jax
import jax.numpy as jnp
from jax import lax
import numpy as np

D_MODEL = 4096
BATCH = 4
SEQ = 2048
DEPTH = 4
DEC_BATCH = 8
DEC_SEQ = 1
PAST_LEN = 8192
PAGE_SIZE = 128

N_AB_LAYERS = (DEPTH + 1) // 2
N_C_LAYERS = DEPTH // 2
FFN_DIM = ((8 * D_MODEL // 3 + 255) // 256) * 256
PLE_DIM = 256
NORM_EPS = 1e-6
CONV_CH = D_MODEL // 2
CONV_K = 31
SSM_D_INNER = D_MODEL
SSM_HEAD_DIM = 64
SSM_HEADS = SSM_D_INNER // SSM_HEAD_DIM
SSM_GROUPS = 8
SSM_STATE = 128
SSM_CONV_K = 4
SSM_CHUNK = 128
SSM_XBC = SSM_D_INNER + 2 * SSM_GROUPS * SSM_STATE
AB_IN = 2 * CONV_CH + SSM_D_INNER + SSM_XBC + SSM_HEADS
AB_OUT = CONV_CH + SSM_D_INNER
NSA_HEAD_DIM = 128
NSA_HEADS = D_MODEL // NSA_HEAD_DIM
NSA_KV_HEADS = 4
NSA_GROUP = NSA_HEADS // NSA_KV_HEADS
N_KV_SLOTS = 4
CMP_LEN = 32
CMP_STRIDE = 16
SEL_BLOCK = 64
SEL_TOPK = 16
WINDOW = 512
NSA_QBLK = 32
WIN_QBLK = 128
NSA_QDIM = NSA_HEADS * NSA_HEAD_DIM
NSA_IN = NSA_QDIM + (N_KV_SLOTS + 2) * NSA_KV_HEADS * NSA_HEAD_DIM + 3 * NSA_HEADS
SCALE = NSA_HEAD_DIM ** -0.5
NEG = -1e30
FORCE_SCORE = 1e6

kernel_name = 'hybrid_conv_ssd_nsa_decoder_step'

f32 = jnp.float32


def _rms_norm(x, g):
    xf = x.astype(f32)
    y = xf * lax.rsqrt(jnp.mean(xf * xf, axis=-1, keepdims=True) + NORM_EPS)
    return (y * g.astype(f32)).astype(x.dtype)


def _layer_norm(x, g, b):
    xf = x.astype(f32)
    xc = xf - jnp.mean(xf, axis=-1, keepdims=True)
    var = jnp.mean(xc * xc, axis=-1, keepdims=True)
    return (xc * lax.rsqrt(var + NORM_EPS) * g.astype(f32) + b.astype(f32)).astype(x.dtype)


def _swiglu(h, w_up, w_down):
    a, b = jnp.split(h @ w_up, 2, axis=-1)
    return (jax.nn.silu(a) * b) @ w_down


def _causal_depthwise(x_ext, w, b):
    y = lax.conv_general_dilated(x_ext, w[:, None, :].astype(x_ext.dtype), (1,), 'VALID',
                                 dimension_numbers=('NWC', 'WIO', 'NWC'),
                                 feature_group_count=x_ext.shape[-1])
    return y + b


def _masked_softmax(s, mask):
    s = jnp.where(mask, s, NEG)
    m = jnp.max(s, axis=-1, keepdims=True)
    e = jnp.where(mask, jnp.exp(s - m), 0.0)
    return e / jnp.maximum(jnp.sum(e, axis=-1, keepdims=True), 1e-30)


def _block_len(t, blk):
    return blk if t % blk == 0 else t


def _split_queries(q, qpos, blk):
    bsz, g, e, t, d = q.shape
    nb = t // blk
    return jnp.moveaxis(q.reshape(bsz, g, e, nb, blk, d), 3, 0), qpos.reshape(nb, blk)


def _merge_queries(o):
    nb, bsz, g, e, blk, d = o.shape
    return jnp.moveaxis(o, 0, 3).reshape(bsz, g, e, nb * blk, d)


def _gather_pages(pool, page_table):
    rows = pool[page_table]
    return rows.reshape((page_table.shape[0], page_table.shape[1] * pool.shape[1]) + pool.shape[2:])


def _ssd(x, dt, a, bm, cm, h0):
    bsz, L, H, P = x.shape
    G, N = bm.shape[2], bm.shape[3]
    E = H // G
    Q = _block_len(L, SSM_CHUNK)
    nc = L // Q
    xd = (x.astype(f32) * dt[..., None]).reshape(bsz, nc, Q, G, E, P)
    la = (dt * a).reshape(bsz, nc, Q, G, E)
    bmc = bm.astype(f32).reshape(bsz, nc, Q, G, N)
    cmc = cm.astype(f32).reshape(bsz, nc, Q, G, N)
    cs = jnp.cumsum(la, axis=2)
    causal = jnp.tril(jnp.ones((Q, Q), bool))
    seg = cs[:, :, :, None] - cs[:, :, None, :]
    decay = jnp.exp(jnp.where(causal[None, None, :, :, None, None], seg, -jnp.inf))
    cb = jnp.einsum('bclgn,bcsgn->bclsg', cmc, bmc)
    y_diag = jnp.einsum('bclsge,bcsgep->bclgep', cb[..., None] * decay, xd)
    xw = xd * jnp.exp(cs[:, :, -1:] - cs)[..., None]
    chunk_states = jnp.einsum('bcsgn,bcsgep->bcgepn', bmc, xw)
    chunk_decay = jnp.exp(cs[:, :, -1])

    def step(h, inp):
        st, dec = inp
        return h * dec[..., None, None] + st, h

    h_fin, h_start = lax.scan(step, h0.astype(f32).reshape(bsz, G, E, P, N),
                              (jnp.moveaxis(chunk_states, 1, 0), jnp.moveaxis(chunk_decay, 1, 0)))
    y_off = jnp.einsum('bclgn,cbgepn->bclgep', cmc, h_start) * jnp.exp(cs)[..., None]
    y = (y_diag + y_off).reshape(bsz, L, H, P)
    return y, h_fin.reshape(bsz, H, P, N)


def _conv_ssd_mixer(h, conv_buf, ssm_conv_buf, ssm_h, w_in, w_out, dw_w, dw_b, ln_g, ln_b,
                    sc_w, sc_b, dt_bias, a_log, d_skip, norm_g):
    bsz, T, _ = h.shape
    proj = h @ w_in
    o1 = CONV_CH
    o2 = 2 * CONV_CH
    o3 = o2 + SSM_D_INNER
    o4 = o3 + SSM_XBC
    u_a, u_g, z, xbc, dt_raw = jnp.split(proj, [o1, o2, o3, o4], axis=-1)
    u = u_a * jax.nn.sigmoid(u_g)
    u_ext = jnp.concatenate([conv_buf.astype(u.dtype), u], axis=1)
    c = jax.nn.silu(_layer_norm(_causal_depthwise(u_ext, dw_w, dw_b), ln_g, ln_b))
    xbc_ext = jnp.concatenate([ssm_conv_buf.astype(xbc.dtype), xbc], axis=1)
    xbc_c = jax.nn.silu(_causal_depthwise(xbc_ext, sc_w, sc_b))
    xs, bm, cm = jnp.split(xbc_c, [SSM_D_INNER, SSM_D_INNER + SSM_GROUPS * SSM_STATE], axis=-1)
    xs = xs.reshape(bsz, T, SSM_HEADS, SSM_HEAD_DIM)
    dt = jax.nn.softplus(dt_raw.astype(f32) + dt_bias.astype(f32))
    a = -jnp.exp(a_log.astype(f32))
    y, h_new = _ssd(xs, dt, a, bm.reshape(bsz, T, SSM_GROUPS, SSM_STATE),
                    cm.reshape(bsz, T, SSM_GROUPS, SSM_STATE), ssm_h)
    y = y + d_skip.astype(f32)[:, None] * xs.astype(f32)
    yg = (y.reshape(bsz, T, SSM_D_INNER) * jax.nn.silu(z.astype(f32))).reshape(bsz, T, SSM_GROUPS, -1)
    yg = yg * lax.rsqrt(jnp.mean(yg * yg, axis=-1, keepdims=True) + NORM_EPS)
    y = (yg.reshape(bsz, T, SSM_D_INNER) * norm_g.astype(f32)).astype(h.dtype)
    out = jnp.concatenate([c, y], axis=-1) @ w_out
    return out, u_ext[:, -(CONV_K - 1):], xbc_ext[:, -(SSM_CONV_K - 1):], h_new.astype(ssm_h.dtype)


def _compress(rows, w):
    n_sub = rows.shape[1] // CMP_STRIDE
    sub = rows[:, :n_sub * CMP_STRIDE].reshape((rows.shape[0], n_sub, CMP_STRIDE) + rows.shape[2:])
    w = w.reshape((CMP_LEN // CMP_STRIDE, CMP_STRIDE) + w.shape[1:]).astype(rows.dtype)
    head = jnp.einsum('bnjgd,jgd->bngd', sub, w[0])
    tail = jnp.einsum('bnjgd,jgd->bngd', sub, w[1])
    return head[:, :-1] + tail[:, 1:]


def _nsa_mixer(h, past_kv, win_buf, w_in, w_out, cmp_w):
    bsz, T, _ = h.shape
    P = past_kv.shape[1]
    PB = win_buf.shape[1]
    KVH, E, HD = NSA_KV_HEADS, NSA_GROUP, NSA_HEAD_DIM
    proj = h @ w_in
    o1 = NSA_QDIM
    o2 = o1 + N_KV_SLOTS * KVH * HD
    o3 = o2 + 2 * KVH * HD
    q, kv_new, win_new, gate = jnp.split(proj, [o1, o2, o3], axis=-1)
    q = q.reshape(bsz, T, KVH, E, HD).transpose(0, 2, 3, 1, 4)
    kv_new = kv_new.reshape(bsz, T, N_KV_SLOTS, KVH, HD)
    win_new = win_new.reshape(bsz, T, 2, KVH, HD)
    gate = jax.nn.sigmoid(gate.astype(f32)).reshape(bsz, T, 3, KVH, E).transpose(2, 0, 3, 4, 1)
    slopes = jnp.exp2(-8.0 * jnp.arange(1, NSA_HEADS + 1, dtype=f32) / NSA_HEADS).reshape(KVH, E)
    qpos = P + jnp.arange(T)
    full = jnp.concatenate([past_kv.astype(kv_new.dtype), kv_new], axis=1)
    L = P + T
    k_cmp = _compress(full[:, :, 0], cmp_w[0])
    v_cmp = _compress(full[:, :, 1], cmp_w[1])
    n_cmp = k_cmp.shape[1]
    c_end = jnp.arange(n_cmp) * CMP_STRIDE + CMP_LEN - 1
    n_sel = -(-L // SEL_BLOCK)
    sel = jnp.pad(full[:, :, 2:], ((0, 0), (0, n_sel * SEL_BLOCK - L), (0, 0), (0, 0), (0, 0)))
    sel = sel.reshape(bsz, n_sel, SEL_BLOCK, 2, KVH, HD).transpose(3, 0, 4, 1, 2, 5)
    k_sel, v_sel = sel[0], sel[1]
    k_top = min(SEL_TOPK, n_sel)
    ratio = SEL_BLOCK // CMP_STRIDE
    gather = jax.vmap(jax.vmap(lambda blocks, idx: blocks[idx]))

    def cmp_sel_block(args):
        qb, qp = args
        qn = qb.shape[3]
        dist = (qp[:, None] - c_end[None, :])
        s = jnp.einsum('bgeqd,bngd->bgeqn', qb, k_cmp, preferred_element_type=f32) * SCALE
        s = s - slopes[:, :, None, None] * dist.astype(f32)
        p_c = _masked_softmax(s, (dist >= 0)[None, None, None])
        o_c = jnp.einsum('bgeqn,bngd->bgeqd', p_c.astype(v_cmp.dtype), v_cmp)
        imp = jnp.sum(p_c, axis=2)
        imp = jnp.pad(imp, ((0, 0), (0, 0), (0, 0), (0, n_sel * ratio - n_cmp)))
        imp = imp.reshape(bsz, KVH, qn, n_sel, ratio).sum(-1)
        j = jnp.arange(n_sel)[None, :]
        cur = (qp // SEL_BLOCK)[:, None]
        valid = j * SEL_BLOCK <= qp[:, None]
        forced = (j == 0) | (j == cur) | (j == cur - 1)
        score = jnp.where(valid, jnp.where(forced, FORCE_SCORE, imp), -FORCE_SCORE)
        _, idx = lax.top_k(score, k_top)
        kg = gather(k_sel, idx)
        vg = gather(v_sel, idx)
        s = jnp.einsum('bgeqd,bgqkld->bgeqkl', qb, kg, preferred_element_type=f32) * SCALE
        kpos = idx[..., None] * SEL_BLOCK + jnp.arange(SEL_BLOCK)
        dsel = (qp[None, None, :, None, None] - kpos)[:, :, None]
        s = s - slopes[None, :, :, None, None, None] * dsel.astype(f32)
        p_s = _masked_softmax(s.reshape(bsz, KVH, E, qn, -1), (dsel >= 0).reshape(bsz, KVH, 1, qn, -1))
        o_s = jnp.einsum('bgeqm,bgqmd->bgeqd', p_s.astype(vg.dtype), vg.reshape(bsz, KVH, qn, -1, HD))
        return o_c, o_s

    qblk = _block_len(T, NSA_QBLK)
    qb_all, qp_all = _split_queries(q, qpos, qblk)
    o_c, o_s = lax.map(cmp_sel_block, (qb_all, qp_all))
    o_c, o_s = _merge_queries(o_c), _merge_queries(o_s)

    k_ext = jnp.concatenate([win_buf[:, :, 0].astype(win_new.dtype), win_new[:, :, 0]], axis=1)
    v_ext = jnp.concatenate([win_buf[:, :, 1].astype(win_new.dtype), win_new[:, :, 1]], axis=1)
    pos_ext = P - PB + jnp.arange(PB + T)
    wblk = _block_len(T, WIN_QBLK)

    def win_block(args):
        bi, qb, qp = args
        ks = lax.dynamic_slice_in_dim(k_ext, bi * wblk, PB + wblk, axis=1)
        vs = lax.dynamic_slice_in_dim(v_ext, bi * wblk, PB + wblk, axis=1)
        kp = lax.dynamic_slice_in_dim(pos_ext, bi * wblk, PB + wblk)
        dist = qp[:, None] - kp[None, :]
        s = jnp.einsum('bgeqd,bsgd->bgeqs', qb, ks, preferred_element_type=f32) * SCALE
        s = s - slopes[:, :, None, None] * dist.astype(f32)
        mask = (dist >= 0) & (dist < WINDOW) & (kp >= 0)[None, :]
        p = _masked_softmax(s, mask)
        return jnp.einsum('bgeqs,bsgd->bgeqd', p.astype(vs.dtype), vs)

    qw_all, pw_all = _split_queries(q, qpos, wblk)
    o_w = _merge_queries(lax.map(win_block, (jnp.arange(T // wblk), qw_all, pw_all)))

    o = gate[0][..., None] * o_c + gate[1][..., None] * o_s + gate[2][..., None] * o_w
    o = o.transpose(0, 3, 1, 2, 4).reshape(bsz, T, NSA_QDIM).astype(h.dtype)
    out = o @ w_out
    new_win = jnp.concatenate([win_buf.astype(win_new.dtype), win_new], axis=1)[:, -min(WINDOW, L):]
    return out, kv_new, new_win


def _trunk(x, p, past_kv, win_buf, conv_buf, ssm_conv_buf, ssm_h, prm):
    kv_rows, win_new, conv_new, ssm_conv_new, ssm_new = [], [], [], [], []
    for i in range(DEPTH):
        j = i // 2
        g = prm['norm_g'][i]
        x = x + 0.5 * _swiglu(_rms_norm(x, g[0]), prm['ffn_w_up'][i, 0], prm['ffn_w_down'][i, 0])
        h = _rms_norm(x, g[1])
        if i % 2 == 0:
            out, cb, scb, sh = _conv_ssd_mixer(
                h, conv_buf[j], ssm_conv_buf[j], ssm_h[j], prm['ab_w_in'][j], prm['ab_w_out'][j],
                prm['conv_dw_w'][j], prm['conv_dw_b'][j], prm['conv_ln_g'][j], prm['conv_ln_b'][j],
                prm['ssm_conv_w'][j], prm['ssm_conv_b'][j], prm['ssm_dt_bias'][j], prm['ssm_a_log'][j],
                prm['ssm_d'][j], prm['ssm_norm_g'][j])
            conv_new.append(cb)
            ssm_conv_new.append(scb)
            ssm_new.append(sh)
        else:
            out, rows, wb = _nsa_mixer(h, past_kv[j], win_buf[j], prm['nsa_w_in'][j],
                                       prm['nsa_w_out'][j], prm['nsa_cmp_w'][j])
            kv_rows.append(rows)
            win_new.append(wb)
        x = x + out
        x = x + 0.5 * _swiglu(_rms_norm(x, g[2]), prm['ffn_w_up'][i, 1], prm['ffn_w_down'][i, 1])
        ple_gate = jax.nn.sigmoid(_rms_norm(x, g[3]) @ prm['ple_w_gate'][i])
        x = x + ple_gate * (p[i].astype(x.dtype) @ prm['ple_w_proj'][i])
    y = _rms_norm(x, prm['final_norm_g'])
    return (y, jnp.stack(kv_rows), jnp.stack(win_new), jnp.stack(conv_new),
            jnp.stack(ssm_conv_new), jnp.stack(ssm_new))


def setup_inputs(seed: int = 0) -> dict:
    key = jax.random.key(seed)
    counter = [0]

    def nk():
        counter[0] += 1
        return jax.random.fold_in(key, counter[0])

    def nrm(shape, scale=1.0):
        return jax.random.normal(nk(), shape, f32) * scale

    def gain(shape):
        return 1.0 + nrm(shape, 0.1)

    n_pages = PAST_LEN // PAGE_SIZE
    n_pool = (5 * DEC_BATCH * n_pages + 3) // 4
    win_rows = min(WINDOW, PAST_LEN)
    kv = (NSA_KV_HEADS, NSA_HEAD_DIM)
    page_table = jax.random.permutation(nk(), n_pool)[: DEC_BATCH * n_pages]
    page_table = page_table.reshape(DEC_BATCH, n_pages).astype(jnp.int32)
    dt0 = jnp.exp(jax.random.uniform(nk(), (N_AB_LAYERS, SSM_HEADS), f32, math.log(1e-3), math.log(1e-1)))
    ssm_dt_bias = dt0 + jnp.log(-jnp.expm1(-dt0))
    ssm_a_log = jnp.log(jax.random.uniform(nk(), (N_AB_LAYERS, SSM_HEADS), f32, 1.0, 16.0))
    return {
        'x_prompt': nrm((BATCH, SEQ, D_MODEL)),
        'x_sample': nrm((DEC_BATCH, DEC_SEQ, D_MODEL)),
        'cache_nsa_kv': nrm((N_C_LAYERS, n_pool, PAGE_SIZE, N_KV_SLOTS) + kv),
        'cache_nsa_win': nrm((N_C_LAYERS, DEC_BATCH, win_rows, 2) + kv),
        'state_conv': nrm((N_AB_LAYERS, DEC_BATCH, CONV_K - 1, CONV_CH), 0.5),
        'state_ssm_conv': nrm((N_AB_LAYERS, DEC_BATCH, SSM_CONV_K - 1, SSM_XBC)),
        'state_ssm': nrm((N_AB_LAYERS, DEC_BATCH, SSM_HEADS, SSM_HEAD_DIM, SSM_STATE), 0.5),
        'page_table': page_table,
        'p_prompt': nrm((DEPTH, BATCH, SEQ, PLE_DIM)),
        'p_sample': nrm((DEPTH, DEC_BATCH, DEC_SEQ, PLE_DIM)),
        'norm_g': gain((DEPTH, 4, D_MODEL)),
        'final_norm_g': gain((D_MODEL,)),
        'ffn_w_up': nrm((DEPTH, 2, D_MODEL, 2 * FFN_DIM), D_MODEL ** -0.5),
        'ffn_w_down': nrm((DEPTH, 2, FFN_DIM, D_MODEL), FFN_DIM ** -0.5),
        'ple_w_gate': nrm((DEPTH, D_MODEL, D_MODEL), D_MODEL ** -0.5),
        'ple_w_proj': nrm((DEPTH, PLE_DIM, D_MODEL), PLE_DIM ** -0.5),
        'ab_w_in': nrm((N_AB_LAYERS, D_MODEL, AB_IN), D_MODEL ** -0.5),
        'ab_w_out': nrm((N_AB_LAYERS, AB_OUT, D_MODEL), AB_OUT ** -0.5),
        'conv_dw_w': nrm((N_AB_LAYERS, CONV_K, CONV_CH), CONV_K ** -0.5),
        'conv_dw_b': nrm((N_AB_LAYERS, CONV_CH), 0.02),
        'conv_ln_g': gain((N_AB_LAYERS, CONV_CH)),
        'conv_ln_b': nrm((N_AB_LAYERS, CONV_CH), 0.02),
        'ssm_conv_w': nrm((N_AB_LAYERS, SSM_CONV_K, SSM_XBC), SSM_CONV_K ** -0.5),
        'ssm_conv_b': nrm((N_AB_LAYERS, SSM_XBC), 0.02),
        'ssm_dt_bias': ssm_dt_bias,
        'ssm_a_log': ssm_a_log,
        'ssm_d': gain((N_AB_LAYERS, SSM_HEADS)),
        'ssm_norm_g': gain((N_AB_LAYERS, SSM_D_INNER)),
        'nsa_w_in': nrm((N_C_LAYERS, D_MODEL, NSA_IN), D_MODEL ** -0.5),
        'nsa_w_out': nrm((N_C_LAYERS, NSA_QDIM, D_MODEL), NSA_QDIM ** -0.5),
        'nsa_cmp_w': (1.0 + nrm((N_C_LAYERS, 2, CMP_LEN) + kv, 0.5)) * CMP_LEN ** -0.5,
    }


def reference(x_prompt, x_sample, cache_nsa_kv, cache_nsa_win, state_conv, state_ssm_conv, state_ssm,
              page_table, p_prompt, p_sample, norm_g, final_norm_g, ffn_w_up, ffn_w_down, ple_w_gate,
              ple_w_proj, ab_w_in, ab_w_out, conv_dw_w, conv_dw_b, conv_ln_g, conv_ln_b, ssm_conv_w,
              ssm_conv_b, ssm_dt_bias, ssm_a_log, ssm_d, ssm_norm_g, nsa_w_in, nsa_w_out, nsa_cmp_w):
    prm = {
        'norm_g': norm_g, 'final_norm_g': final_norm_g, 'ffn_w_up': ffn_w_up, 'ffn_w_down': ffn_w_down,
        'ple_w_gate': ple_w_gate, 'ple_w_proj': ple_w_proj, 'ab_w_in': ab_w_in, 'ab_w_out': ab_w_out,
        'conv_dw_w': conv_dw_w, 'conv_dw_b': conv_dw_b, 'conv_ln_g': conv_ln_g, 'conv_ln_b': conv_ln_b,
        'ssm_conv_w': ssm_conv_w, 'ssm_conv_b': ssm_conv_b, 'ssm_dt_bias': ssm_dt_bias,
        'ssm_a_log': ssm_a_log, 'ssm_d': ssm_d, 'ssm_norm_g': ssm_norm_g,
        'nsa_w_in': nsa_w_in, 'nsa_w_out': nsa_w_out, 'nsa_cmp_w': nsa_cmp_w,
    }
    b, dt = x_prompt.shape[0], x_prompt.dtype
    kv = (NSA_KV_HEADS, NSA_HEAD_DIM)
    y_p, kv_p, win_p, conv_p, sconv_p, ssm_p = _trunk(
        x_prompt, p_prompt,
        [jnp.zeros((b, 0, N_KV_SLOTS) + kv, dt)] * N_C_LAYERS,
        jnp.zeros((N_C_LAYERS, b, WINDOW, 2) + kv, dt),
        jnp.zeros((N_AB_LAYERS, b, CONV_K - 1, CONV_CH), dt),
        jnp.zeros((N_AB_LAYERS, b, SSM_CONV_K - 1, SSM_XBC), dt),
        jnp.zeros((N_AB_LAYERS, b, SSM_HEADS, SSM_HEAD_DIM, SSM_STATE), dt),
        prm)
    past = [_gather_pages(cache_nsa_kv[j], page_table) for j in range(N_C_LAYERS)]
    y_s, kv_s, win_s, conv_s, sconv_s, ssm_s = _trunk(
        x_sample, p_sample, past, cache_nsa_win, state_conv, state_ssm_conv, state_ssm, prm)
    return (y_p, y_s, kv_p, kv_s, win_p, win_s, conv_p, conv_s, sconv_p, sconv_s, ssm_p, ssm_s)
```

```python
import functools
import math

import jax
import jax.numpy as jnp
from jax import lax
from jax.experimental import pallas as pl
from jax.experimental.pallas import tpu as pltpu

f32 = jnp.float32
bf16 = jnp.bfloat16

V7X_LANES = 128
V7X_SUBLANES = 8
V7X_VMEM_BYTES = 64 * 1024 * 1024
VMEM_BUDGET = 40 * 1024 * 1024
VMEM_LIMIT = 56 * 1024 * 1024

NORM_EPS = 1e-6
CMP_STRIDE = 16
SEL_BLOCK = 64
SEL_TOPK = 16
SSM_CHUNK = 128
NEG = -1e30
FORCE_SCORE = 1e6


def _cparams(sem):
    return pltpu.CompilerParams(dimension_semantics=sem, vmem_limit_bytes=VMEM_LIMIT)


def _sigmoid(x):
    return jax.nn.sigmoid(x)


def _softplus(x):
    return jnp.maximum(x, 0.0) + jnp.log(1.0 + jnp.exp(-jnp.abs(x)))


def _mm_kernel(*refs, n_x, n_w, n_extra, n_acc, terms, epilogue, nk):
    xs = refs[:n_x]
    ws = refs[n_x:n_x + n_w]
    extras = refs[n_x + n_w:n_x + n_w + n_extra]
    out = refs[n_x + n_w + n_extra]
    accs = refs[n_x + n_w + n_extra + 1:]

    def partials():
        res = [None] * n_acc
        for xi, wi, ai in terms:
            d = jnp.dot(xs[xi][...], ws[wi][...], preferred_element_type=f32)
            res[ai] = d if res[ai] is None else res[ai] + d
        return res

    if nk == 1:
        out[...] = epilogue(partials(), [e[...] for e in extras]).astype(out.dtype)
        return

    k = pl.program_id(2)
    parts = partials()

    @pl.when(k == 0)
    def _():
        for a, p in zip(accs, parts):
            a[...] = p

    @pl.when(jnp.logical_and(k > 0, k < nk - 1))
    def _():
        for a, p in zip(accs, parts):
            a[...] += p

    @pl.when(k == nk - 1)
    def _():
        full = [a[...] + p for a, p in zip(accs, parts)]
        out[...] = epilogue(full, [e[...] for e in extras]).astype(out.dtype)


def _pick_bn(n, cands):
    for c in cands:
        if n % c == 0:
            return c
    raise ValueError(f"no column block for N={n}")


def _matmul(xs, ws, terms, n_acc, epilogue, n_cols, *, extras=(), out_dtype=f32, nk=1, bn_cap=1024):
    m = xs[0].shape[0]
    bm = m if m <= 1024 else 1024
    assert m % bm == 0
    xbytes = sum(bm * (x.shape[1] // nk) * x.dtype.itemsize for x in xs)

    def est(bn):
        wbytes = sum(w.shape[0] // nk * bn * w.dtype.itemsize for w, _ in ws)
        ebytes = sum((bm if kind == "mn" else 1) * bn * e.dtype.itemsize for e, kind in extras)
        obytes = bm * bn * jnp.dtype(out_dtype).itemsize
        abytes = n_acc * bm * bn * 4 * (2 if nk > 1 else 1)
        return 2 * (xbytes + wbytes + ebytes + obytes) + abytes

    cands = [c for c in (2048, 1024, 512, 256, 128) if c <= bn_cap and n_cols % c == 0]
    bn = next((c for c in cands if est(c) <= VMEM_BUDGET), cands[-1])
    grid = (m // bm, n_cols // bn, nk)

    in_specs = []
    for x in xs:
        in_specs.append(pl.BlockSpec((bm, x.shape[1] // nk), lambda i, j, k: (i, k)))
    w_arrays = []
    for w, off_cols in ws:
        assert off_cols % bn == 0
        off = off_cols // bn
        in_specs.append(pl.BlockSpec((w.shape[0] // nk, bn), lambda i, j, k, off=off: (k, j + off)))
        w_arrays.append(w)
    e_arrays = []
    for e, kind in extras:
        if kind == "mn":
            in_specs.append(pl.BlockSpec((bm, bn), lambda i, j, k: (i, j)))
        else:
            in_specs.append(pl.BlockSpec((1, bn), lambda i, j, k: (0, j)))
        e_arrays.append(e)
    scratch = [pltpu.VMEM((bm, bn), f32) for _ in range(n_acc)] if nk > 1 else []
    kern = functools.partial(_mm_kernel, n_x=len(xs), n_w=len(ws), n_extra=len(extras), n_acc=n_acc,
                             terms=tuple(terms), epilogue=epilogue, nk=nk)
    return pl.pallas_call(
        kern,
        out_shape=jax.ShapeDtypeStruct((m, n_cols), out_dtype),
        grid=grid,
        in_specs=in_specs,
        out_specs=pl.BlockSpec((bm, bn), lambda i, j, k: (i, j)),
        scratch_shapes=scratch,
        compiler_params=_cparams(("parallel", "parallel", "arbitrary")),
        name="mm",
    )(*xs, *w_arrays, *e_arrays)


def _ep_plain(accs, extras):
    return accs[0]


def _ep_swiglu(accs, extras):
    a, b = accs
    return a * _sigmoid(a) * b


def _ep_glu(accs, extras):
    return accs[0] * _sigmoid(accs[1])


def _ep_sigmoid(accs, extras):
    return _sigmoid(accs[0])


def _ep_softplus_bias(accs, extras):
    return _softplus(accs[0] + extras[0])


def _ep_half_resid(accs, extras):
    return extras[0] + 0.5 * accs[0]


def _ep_resid(accs, extras):
    return extras[0] + accs[0]


def _ep_ple(accs, extras):
    return extras[0] + _sigmoid(accs[0]) * extras[1]


def _rmsnorm_kernel(x_ref, g_ref, o_ref):
    x = x_ref[...]
    y = x * lax.rsqrt(jnp.mean(x * x, axis=-1, keepdims=True) + NORM_EPS)
    o_ref[...] = (y * g_ref[...]).astype(o_ref.dtype)


def _ln_silu_kernel(x_ref, g_ref, b_ref, o_ref):
    x = x_ref[...]
    xc = x - jnp.mean(x, axis=-1, keepdims=True)
    var = jnp.mean(xc * xc, axis=-1, keepdims=True)
    y = xc * lax.rsqrt(var + NORM_EPS) * g_ref[...] + b_ref[...]
    o_ref[...] = (y * _sigmoid(y)).astype(o_ref.dtype)


def _rowwise(kern, x, vecs, out_dtype, name):
    m, d = x.shape
    bm = m if m <= 256 else 256
    assert m % bm == 0
    return pl.pallas_call(
        kern,
        out_shape=jax.ShapeDtypeStruct((m, d), out_dtype),
        grid=(m // bm,),
        in_specs=[pl.BlockSpec((bm, d), lambda i: (i, 0))] + [pl.BlockSpec((1, d), lambda i: (0, 0)) for _ in vecs],
        out_specs=pl.BlockSpec((bm, d), lambda i: (i, 0)),
        compiler_params=_cparams(("parallel",)),
        name=name,
    )(x, *[v.reshape(1, d).astype(f32) for v in vecs])


def _rmsnorm(x, g, out_dtype=bf16):
    return _rowwise(_rmsnorm_kernel, x, [g], out_dtype, "rmsnorm")


def _ln_silu(x, g, b):
    return _rowwise(_ln_silu_kernel, x, [g, b], bf16, "ln_silu")


def _conv_kernel(u_ref, st_ref, w_ref, b_ref, o_ref, win_ref, *, taps, halo, tt, silu):
    j = pl.program_id(2)

    @pl.when(j == 0)
    def _():
        win_ref[0:halo, :] = st_ref[0]

    win_ref[halo:halo + tt, :] = u_ref[...]
    off = halo - (taps - 1)
    bias = b_ref[...]
    for r in range(tt // V7X_SUBLANES):
        r0 = r * V7X_SUBLANES
        acc = jnp.broadcast_to(bias, (V7X_SUBLANES, bias.shape[1]))
        for k in range(taps):
            acc = acc + w_ref[k:k + 1, :] * win_ref[r0 + off + k:r0 + off + k + V7X_SUBLANES, :]
        if silu:
            acc = acc * _sigmoid(acc)
        o_ref[r0:r0 + V7X_SUBLANES, :] = acc.astype(o_ref.dtype)
    win_ref[0:halo, :] = win_ref[tt:tt + halo, :]


def _causal_conv(u, state, w, b, *, batch, silu, out_dtype=f32):
    m, c = u.shape
    t = m // batch
    taps = w.shape[0]
    halo = -(-(taps - 1) // V7X_SUBLANES) * V7X_SUBLANES
    tt = t if t <= 256 else 256
    assert t % tt == 0 and tt % V7X_SUBLANES == 0
    cc = 512 if c % 512 == 0 else V7X_LANES
    st = jnp.pad(state.astype(f32), ((0, 0), (halo - (taps - 1), 0), (0, 0)))
    nt = t // tt
    kern = functools.partial(_conv_kernel, taps=taps, halo=halo, tt=tt, silu=silu)
    return pl.pallas_call(
        kern,
        out_shape=jax.ShapeDtypeStruct((m, c), out_dtype),
        grid=(batch, c // cc, nt),
        in_specs=[
            pl.BlockSpec((tt, cc), lambda bi, ci, j: (bi * nt + j, ci)),
            pl.BlockSpec((1, halo, cc), lambda bi, ci, j: (bi, 0, ci)),
            pl.BlockSpec((taps, cc), lambda bi, ci, j: (0, ci)),
            pl.BlockSpec((1, cc), lambda bi, ci, j: (0, ci)),
        ],
        out_specs=pl.BlockSpec((tt, cc), lambda bi, ci, j: (bi * nt + j, ci)),
        scratch_shapes=[pltpu.VMEM((halo + tt, cc), f32)],
        compiler_params=_cparams(("parallel", "parallel", "arbitrary")),
        name="causal_conv",
    )(u, st, w.astype(f32), b.reshape(1, c).astype(f32))


def _split3(x):
    hi = x.astype(bf16)
    r1 = x - hi.astype(f32)
    mid = r1.astype(bf16)
    lo = (r1 - mid.astype(f32)).astype(bf16)
    return hi, mid, lo


def _dot_nt(a, b):
    return lax.dot_general(a, b, (((1,), (1,)), ((), ())), preferred_element_type=f32)


def _dot_tn(a, b):
    return lax.dot_general(a, b, (((0,), (0,)), ((), ())), preferred_element_type=f32)


def _ssd_kernel(xs_ref, b_ref, c_ref, dt_ref, z_ref, a_ref, d_ref, ng_ref, h0_ref, y_ref, hout_ref, h_scr,
                *, q, e_heads, p_dim, nc):
    c = pl.program_id(2)

    @pl.when(c == 0)
    def _():
        h_scr[...] = h0_ref[0]

    xs = xs_ref[...]
    bb = b_ref[...].astype(bf16)
    cb16 = c_ref[...].astype(bf16)
    dt = dt_ref[...]
    la = dt * a_ref[0]
    ri = lax.broadcasted_iota(jnp.int32, (q, q), 0)
    ci = lax.broadcasted_iota(jnp.int32, (q, q), 1)
    causal = ri >= ci
    tri = jnp.where(causal, 1.0, 0.0).astype(bf16)
    hi, mid, lo = _split3(la)
    cs = (jnp.dot(tri, hi, preferred_element_type=f32) + jnp.dot(tri, mid, preferred_element_type=f32)
          + jnp.dot(tri, lo, preferred_element_type=f32))
    cs_t = cs.T
    cb = _dot_nt(cb16, bb)
    dsk = d_ref[0]
    ys = []
    for e in range(e_heads):
        col = cs[:, e:e + 1]
        row = cs_t[e:e + 1, :]
        decay = jnp.exp(jnp.where(causal, col - row, NEG))
        x_e = xs[:, e * p_dim:(e + 1) * p_dim]
        xd = x_e * dt[:, e:e + 1]
        y = jnp.dot((cb * decay).astype(bf16), xd.astype(bf16), preferred_element_type=f32)
        last = cs[q - 1:q, e:e + 1]
        xw = (xd * jnp.exp(last - col)).astype(bf16)
        s_new = _dot_tn(xw, bb)
        h = h_scr[e]
        y = y + _dot_nt(cb16, h.astype(bf16)) * jnp.exp(col)
        h_scr[e] = h * jnp.exp(last) + s_new
        ys.append(y + dsk[:, e:e + 1] * x_e)
    y = jnp.concatenate(ys, axis=1)
    z = z_ref[...]
    yg = y * (z * _sigmoid(z))
    yg = yg * lax.rsqrt(jnp.mean(yg * yg, axis=-1, keepdims=True) + NORM_EPS)
    y_ref[...] = (yg * ng_ref[...]).astype(y_ref.dtype)

    @pl.when(c == nc - 1)
    def _():
        hout_ref[0] = h_scr[...]


def _ssd(xbc, dtg, z, a_g, d_g, norm_g, h0, *, batch, groups):
    m = xbc.shape[0]
    t = m // batch
    _, heads, p_dim, n = h0.shape
    d_inner = heads * p_dim
    e_heads = heads // groups
    gw = e_heads * p_dim
    q = SSM_CHUNK
    assert t % q == 0 and gw % V7X_LANES == 0 and n % V7X_LANES == 0
    nc = t // q
    b_off = d_inner // n
    c_off = b_off + groups
    kern = functools.partial(_ssd_kernel, q=q, e_heads=e_heads, p_dim=p_dim, nc=nc)
    return pl.pallas_call(
        kern,
        out_shape=(jax.ShapeDtypeStruct((m, d_inner), bf16), jax.ShapeDtypeStruct((batch, heads, p_dim, n), f32)),
        grid=(batch, groups, nc),
        in_specs=[
            pl.BlockSpec((q, gw), lambda bi, g, c: (bi * nc + c, g)),
            pl.BlockSpec((q, n), lambda bi, g, c: (bi * nc + c, b_off + g)),
            pl.BlockSpec((q, n), lambda bi, g, c: (bi * nc + c, c_off + g)),
            pl.BlockSpec((q, V7X_LANES), lambda bi, g, c: (bi * nc + c, g)),
            pl.BlockSpec((q, gw), lambda bi, g, c: (bi * nc + c, g)),
            pl.BlockSpec((1, 1, V7X_LANES), lambda bi, g, c: (g, 0, 0)),
            pl.BlockSpec((1, 1, V7X_LANES), lambda bi, g, c: (g, 0, 0)),
            pl.BlockSpec((1, gw), lambda bi, g, c: (0, g)),
            pl.BlockSpec((1, e_heads, p_dim, n), lambda bi, g, c: (bi, g, 0, 0)),
        ],
        out_specs=(
            pl.BlockSpec((q, gw), lambda bi, g, c: (bi * nc + c, g)),
            pl.BlockSpec((1, e_heads, p_dim, n), lambda bi, g, c: (bi, g, 0, 0)),
        ),
        scratch_shapes=[pltpu.VMEM((e_heads, p_dim, n), f32)],
        compiler_params=_cparams(("parallel", "parallel", "arbitrary")),
        name="ssd",
    )(xbc, xbc, xbc, dtg, z, a_g, d_g, norm_g.reshape(1, d_inner).astype(f32), h0)


def _log2(v):
    assert v > 0 and v & (v - 1) == 0, v
    return v.bit_length() - 1


def _masked_softmax(s, mask):
    s = jnp.where(mask, s, NEG)
    m = jnp.max(s, axis=-1, keepdims=True)
    e = jnp.where(mask, jnp.exp(s - m), 0.0)
    return e / jnp.maximum(jnp.sum(e, axis=-1, keepdims=True), 1e-30)


def _block_sums(imp, n_lanes):
    n = imp.shape[1]
    shift = _log2(SEL_BLOCK // CMP_STRIDE)
    ii = lax.broadcasted_iota(jnp.int32, (n, n_lanes), 0)
    jj = lax.broadcasted_iota(jnp.int32, (n, n_lanes), 1)
    gm = jnp.where(lax.shift_right_logical(ii, shift) == jj, 1.0, 0.0).astype(bf16)
    hi, mid, lo = _split3(imp)
    return (jnp.dot(hi, gm, preferred_element_type=f32) + jnp.dot(mid, gm, preferred_element_type=f32)
            + jnp.dot(lo, gm, preferred_element_type=f32))


def _compress_partial(rows, wh, wt, stride):
    t, c = rows.shape
    r = rows.reshape(t // stride, stride, c)
    return jnp.sum(r * wh[None], axis=1), jnp.sum(r * wt[None], axis=1)


def _compress_combine(head, tail):
    n, c = head.shape
    nxt = pltpu.roll(tail, n - 1, 0)
    keep = lax.broadcasted_iota(jnp.int32, (n, c), 0) < n - 1
    return head + jnp.where(keep, nxt, 0.0)


def _compress_kernel(rows_ref, wh_ref, wt_ref, o_ref, *, stride):
    head, tail = _compress_partial(rows_ref[...], wh_ref[...], wt_ref[...], stride)
    o_ref[0] = _compress_combine(head, tail)


def _compress(kv, wh, wt, *, batch):
    m = kv.shape[0]
    t = m // batch
    stride, c2 = wh.shape
    cc = 256
    assert c2 % cc == 0 and t % stride == 0
    return pl.pallas_call(
        functools.partial(_compress_kernel, stride=stride),
        out_shape=jax.ShapeDtypeStruct((batch, t // stride, c2), f32),
        grid=(batch, c2 // cc),
        in_specs=[
            pl.BlockSpec((t, cc), lambda bi, ci: (bi, ci)),
            pl.BlockSpec((stride, cc), lambda bi, ci: (0, ci)),
            pl.BlockSpec((stride, cc), lambda bi, ci: (0, ci)),
        ],
        out_specs=pl.BlockSpec((1, t // stride, cc), lambda bi, ci: (bi, 0, ci)),
        compiler_params=_cparams(("parallel", "parallel")),
        name="nsa_compress",
    )(kv, wh, wt)


def _nsa_prompt_kernel(q_ref, ks_ref, vs_ref, kw_ref, vw_ref, kc_ref, vc_ref, gate_ref, o_ref,
                       *, tq, e_heads, hd, n_heads, t, window, cmp_len, tk_sel, tk_win):
    g = pl.program_id(1)
    i = pl.program_id(2)
    q0 = i * tq
    rows = e_heads * tq
    scale = hd ** -0.5
    q = q_ref[...]
    qs = jnp.concatenate([q[:, e * hd:(e + 1) * hd] for e in range(e_heads)], axis=0)
    r_idx = lax.broadcasted_iota(jnp.int32, (rows, 1), 0)
    e_idx = lax.shift_right_logical(r_idx, _log2(tq))
    t_idx = r_idx - e_idx * tq
    slope = jnp.exp2(-8.0 * (g * e_heads + e_idx + 1).astype(f32) / n_heads)
    qpos = q0 + t_idx

    nc = kc_ref.shape[1]
    kc = kc_ref[0].astype(bf16)
    vc = vc_ref[0].astype(bf16)
    s = _dot_nt(qs, kc) * scale
    c_end = lax.broadcasted_iota(jnp.int32, (1, nc), 1) * CMP_STRIDE + (cmp_len - 1)
    dist = qpos - c_end
    s = s - slope * dist.astype(f32)
    p_c = _masked_softmax(s, dist >= 0)
    o_c = jnp.dot(p_c.astype(bf16), vc, preferred_element_type=f32)
    imp = jnp.sum(p_c.reshape(e_heads, tq, nc), axis=0)

    n_sel = -(-t // SEL_BLOCK)
    k_top = min(SEL_TOPK, n_sel)
    bimp = _block_sums(imp, V7X_LANES)
    jl = lax.broadcasted_iota(jnp.int32, (tq, V7X_LANES), 1)
    tpos = q0 + lax.broadcasted_iota(jnp.int32, (tq, V7X_LANES), 0)
    cur = lax.shift_right_logical(tpos, _log2(SEL_BLOCK))
    valid = jl * SEL_BLOCK <= tpos
    forced = (jl == 0) | (jl == cur) | (jl == cur - 1)
    score = jnp.where(valid, jnp.where(forced, FORCE_SCORE, bimp), -FORCE_SCORE)
    rank = jnp.zeros((tq, V7X_LANES), f32)
    for j2 in range(n_sel):
        col = score[:, j2:j2 + 1]
        beats = (col > score) | ((col == score) & (jl > j2))
        rank = rank + jnp.where(beats, 1.0, 0.0)
    sel = jnp.where(rank < k_top, 1.0, 0.0).astype(bf16)

    def flash(k_ref, v_ref, lo, hi, tk, mask_fn):
        def body(kt, carry):
            m, l, acc = carry
            k0 = pl.multiple_of(kt * tk, tk)
            kb = k_ref[pl.ds(k0, tk), :]
            vb = v_ref[pl.ds(k0, tk), :]
            kpos = k0 + lax.broadcasted_iota(jnp.int32, (1, tk), 1)
            d = qpos - kpos
            sc = _dot_nt(qs, kb) * scale - slope * d.astype(f32)
            mask = mask_fn(d, k0)
            sc = jnp.where(mask, sc, NEG)
            m_new = jnp.maximum(m, jnp.max(sc, axis=-1, keepdims=True))
            alpha = jnp.exp(m - m_new)
            p = jnp.where(mask, jnp.exp(sc - m_new), 0.0)
            l = alpha * l + jnp.sum(p, axis=-1, keepdims=True)
            acc = alpha * acc + jnp.dot(p.astype(bf16), vb, preferred_element_type=f32)
            return m_new, l, acc

        init = (jnp.full((rows, 1), NEG, f32), jnp.zeros((rows, 1), f32), jnp.zeros((rows, hd), f32))
        _, l, acc = lax.fori_loop(lo, hi, body, init)
        return acc / jnp.maximum(l, 1e-30)

    def sel_mask(d, k0):
        tk = d.shape[1]
        jj = lax.broadcasted_iota(jnp.int32, (V7X_LANES, tk), 0)
        kk = lax.shift_right_logical(k0 + lax.broadcasted_iota(jnp.int32, (V7X_LANES, tk), 1), _log2(SEL_BLOCK))
        expand = jnp.where(jj == kk, 1.0, 0.0).astype(bf16)
        ms = jnp.dot(sel, expand, preferred_element_type=f32)
        ms = jnp.broadcast_to(ms[None], (e_heads, tq, tk)).reshape(rows, tk)
        return (ms > 0.5) & (d >= 0)

    def win_mask(d, k0):
        return (d >= 0) & (d < window)

    hi_sel = (q0 + tq + tk_sel - 1) // tk_sel
    o_s = flash(ks_ref, vs_ref, 0, hi_sel, tk_sel, sel_mask)
    lo_win = jnp.maximum(q0 - window + 1, 0) // tk_win
    hi_win = (q0 + tq + tk_win - 1) // tk_win
    o_w = flash(kw_ref, vw_ref, lo_win, hi_win, tk_win, win_mask)

    gates = gate_ref[...]
    outs = []
    for e in range(e_heads):
        sl = slice(e * tq, (e + 1) * tq)
        outs.append(gates[:, e:e + 1] * o_c[sl] + gates[:, e_heads + e:e_heads + e + 1] * o_s[sl]
                    + gates[:, 2 * e_heads + e:2 * e_heads + e + 1] * o_w[sl])
    o_ref[...] = jnp.concatenate(outs, axis=1).astype(o_ref.dtype)


def _nsa_prompt(q, kvb, winb, cmp, gate, *, batch, kvh, hd, window, cmp_len):
    m, qd = q.shape
    t = m // batch
    n_heads = qd // hd
    e_heads = n_heads // kvh
    tq = 128
    tk_sel = 256
    tk_win = 128
    assert t % tq == 0 and t % tk_sel == 0 and t >= window and -(-t // SEL_BLOCK) <= V7X_LANES
    nq = t // tq
    nc = cmp.shape[1]
    gw = e_heads * hd
    kern = functools.partial(_nsa_prompt_kernel, tq=tq, e_heads=e_heads, hd=hd, n_heads=n_heads, t=t, window=window,
                             cmp_len=cmp_len, tk_sel=tk_sel, tk_win=tk_win)
    return pl.pallas_call(
        kern,
        out_shape=jax.ShapeDtypeStruct((m, qd), bf16),
        grid=(batch, kvh, nq),
        in_specs=[
            pl.BlockSpec((tq, gw), lambda bi, g, i: (bi * nq + i, g)),
            pl.BlockSpec((t, hd), lambda bi, g, i: (bi, 2 * kvh + g)),
            pl.BlockSpec((t, hd), lambda bi, g, i: (bi, 3 * kvh + g)),
            pl.BlockSpec((t, hd), lambda bi, g, i: (bi, g)),
            pl.BlockSpec((t, hd), lambda bi, g, i: (bi, kvh + g)),
            pl.BlockSpec((1, nc, hd), lambda bi, g, i: (bi, 0, g)),
            pl.BlockSpec((1, nc, hd), lambda bi, g, i: (bi, 0, kvh + g)),
            pl.BlockSpec((tq, V7X_LANES), lambda bi, g, i: (bi * nq + i, g)),
        ],
        out_specs=pl.BlockSpec((tq, gw), lambda bi, g, i: (bi * nq + i, g)),
        compiler_params=_cparams(("parallel", "parallel", "parallel")),
        name="nsa_prompt",
    )(q, kvb, kvb, winb, winb, cmp, cmp, gate)


def _nsa_sample_kernel(pt_ref, *refs, pps, kvh, e_heads, hd, n_heads, past, window, cmp_len, n_steps):
    page_refs = refs[:pps]
    q_ref, kvn_ref, wn_ref, gate_ref, cwin_ref, wh_ref, wt_ref, o_ref = refs[pps:pps + 8]
    head_scr, tail_scr, sel_scr, m_scr, l_scr, acc_scr, oc_scr = refs[pps + 8:]
    ps = pl.program_id(1)
    st = pl.program_id(2)
    scale = hd ** -0.5
    page_rows = page_refs[0].shape[1]
    sub = page_rows // CMP_STRIDE
    n_sub = past // CMP_STRIDE
    n_sel = past // SEL_BLOCK + 1
    k_top = min(SEL_TOPK, n_sel)
    n_sel_pad = sel_scr.shape[2]
    cur = past // SEL_BLOCK

    def slope_of(g):
        e_idx = lax.broadcasted_iota(jnp.int32, (e_heads, 1), 0)
        return jnp.exp2(-8.0 * (g * e_heads + e_idx + 1).astype(f32) / n_heads)

    def q_of(g):
        return q_ref[0, g * e_heads:(g + 1) * e_heads, :].astype(bf16)

    def rnd(x):
        return x.astype(bf16).astype(f32)

    @pl.when(ps == 0)
    def _():
        for r in range(pps):
            head, tail = _compress_partial(page_refs[r][0], wh_ref[...], wt_ref[...], CMP_STRIDE)
            row0 = pl.multiple_of((st * pps + r) * sub, sub)
            head_scr[pl.ds(row0, sub), :] = head
            tail_scr[pl.ds(row0, sub), :] = tail

    @pl.when(jnp.logical_and(ps == 1, st == 0))
    def _():
        cmp = _compress_combine(head_scr[...], tail_scr[...])
        c_end = lax.broadcasted_iota(jnp.int32, (1, n_sub), 1) * CMP_STRIDE + (cmp_len - 1)
        dist = past - c_end
        jl = lax.broadcasted_iota(jnp.int32, (V7X_SUBLANES, n_sel_pad), 1)
        valid = jl * SEL_BLOCK <= past
        forced = (jl == 0) | (jl == cur) | (jl == cur - 1)
        j2 = lax.broadcasted_iota(jnp.int32, (n_sel_pad, n_sel_pad), 0)
        j1 = lax.broadcasted_iota(jnp.int32, (n_sel_pad, n_sel_pad), 1)
        for g in range(kvh):
            kc = cmp[:, g * hd:(g + 1) * hd].astype(bf16)
            vc = cmp[:, (kvh + g) * hd:(kvh + g + 1) * hd].astype(bf16)
            s = _dot_nt(q_of(g), kc) * scale - slope_of(g) * dist.astype(f32)
            p = _masked_softmax(s, dist >= 0)
            oc_scr[g] = jnp.dot(p.astype(bf16), vc, preferred_element_type=f32)
            imp = jnp.sum(p, axis=0, keepdims=True)
            bimp = _block_sums(jnp.broadcast_to(imp, (V7X_SUBLANES, n_sub)), n_sel_pad)
            score = jnp.where(valid, jnp.where(forced, FORCE_SCORE, bimp), -FORCE_SCORE)
            score = jnp.where(jl < n_sel, score, -2.0 * FORCE_SCORE)
            sc_row = jnp.broadcast_to(score[0:1], (n_sel_pad, n_sel_pad))
            sc_col = sc_row.T
            beats = ((sc_col > sc_row) | ((sc_col == sc_row) & (j2 < j1))) & (j2 < n_sel)
            rank = jnp.sum(jnp.where(beats, 1.0, 0.0), axis=0, keepdims=True)
            sel_scr[g] = jnp.broadcast_to(jnp.where(rank < k_top, 1.0, 0.0), (V7X_SUBLANES, n_sel_pad))
        m_scr[...] = jnp.full(m_scr.shape, NEG, f32)
        l_scr[...] = jnp.zeros(l_scr.shape, f32)
        acc_scr[...] = jnp.zeros(acc_scr.shape, f32)

    @pl.when(ps == 1)
    def _():
        for r in range(pps):
            page = page_refs[r][0]
            k0 = (st * pps + r) * page_rows
            kpos = k0 + lax.broadcasted_iota(jnp.int32, (1, page_rows), 1)
            d = past - kpos
            jj = lax.broadcasted_iota(jnp.int32, (n_sel_pad, page_rows), 0)
            kk = lax.shift_right_logical(k0 + lax.broadcasted_iota(jnp.int32, (n_sel_pad, page_rows), 1),
                                         _log2(SEL_BLOCK))
            expand = jnp.where(jj == kk, 1.0, 0.0).astype(bf16)
            for g in range(kvh):
                kb = page[:, g * hd:(g + 1) * hd].astype(bf16)
                vb = page[:, (kvh + g) * hd:(kvh + g + 1) * hd].astype(bf16)
                sc = _dot_nt(q_of(g), kb) * scale - slope_of(g) * d.astype(f32)
                ms = jnp.dot(sel_scr[g].astype(bf16), expand, preferred_element_type=f32)
                mask = (ms[0:1] > 0.5) & (d >= 0)
                sc = jnp.where(mask, sc, NEG)
                m = m_scr[g]
                m_new = jnp.maximum(m, jnp.max(sc, axis=-1, keepdims=True))
                alpha = jnp.exp(m - m_new)
                p = jnp.where(mask, jnp.exp(sc - m_new), 0.0)
                l_scr[g] = alpha * l_scr[g] + jnp.sum(p, axis=-1, keepdims=True)
                acc_scr[g] = alpha * acc_scr[g] + jnp.dot(p.astype(bf16), vb, preferred_element_type=f32)
                m_scr[g] = m_new

    @pl.when(jnp.logical_and(ps == 1, st == n_steps - 1))
    def _():
        cw = cwin_ref[0]
        pb = cw.shape[0]
        wpos = past - pb + lax.broadcasted_iota(jnp.int32, (1, pb), 1)
        dw = past - wpos
        wmask = (dw >= 0) & (dw < window) & (wpos >= 0)
        for g in range(kvh):
            qb = q_of(g)
            q32 = qb.astype(f32)
            slope = slope_of(g)
            kn = rnd(kvn_ref[0, 2 * kvh + g:2 * kvh + g + 1, :])
            vn = rnd(kvn_ref[0, 3 * kvh + g:3 * kvh + g + 1, :])
            sel_new = sel_scr[g][0:1, n_sel - 1:n_sel] > 0.5
            s_n = jnp.where(sel_new, jnp.sum(q32 * kn, axis=-1, keepdims=True) * scale, NEG)
            m = m_scr[g]
            m_new = jnp.maximum(m, s_n)
            alpha = jnp.exp(m - m_new)
            p_n = jnp.where(sel_new, jnp.exp(s_n - m_new), 0.0)
            l = alpha * l_scr[g] + p_n
            acc = alpha * acc_scr[g] + rnd(p_n) * vn
            o_s = acc / jnp.maximum(l, 1e-30)
            kw = cw[:, g * hd:(g + 1) * hd].astype(bf16)
            vw = cw[:, (kvh + g) * hd:(kvh + g + 1) * hd].astype(bf16)
            sw = jnp.where(wmask, _dot_nt(qb, kw) * scale - slope * dw.astype(f32), NEG)
            kwn = rnd(wn_ref[0, g:g + 1, :])
            vwn = rnd(wn_ref[0, kvh + g:kvh + g + 1, :])
            sw_n = jnp.sum(q32 * kwn, axis=-1, keepdims=True) * scale
            mw = jnp.maximum(jnp.max(sw, axis=-1, keepdims=True), sw_n)
            ew = jnp.where(wmask, jnp.exp(sw - mw), 0.0)
            en = jnp.exp(sw_n - mw)
            den = jnp.maximum(jnp.sum(ew, axis=-1, keepdims=True) + en, 1e-30)
            o_w = (jnp.dot(ew.astype(bf16), vw, preferred_element_type=f32) + rnd(en) * vwn) / den
            gates = gate_ref[0, g * e_heads:(g + 1) * e_heads, :]
            o = gates[:, 0:1] * oc_scr[g] + gates[:, 1:2] * o_s + gates[:, 2:3] * o_w
            o_ref[0, g * e_heads:(g + 1) * e_heads, :] = o


def _nsa_sample(q, kvn, wn, gate, pool, cwin, page_table, wh, wt, *, kvh, hd, window, cmp_len):
    batch, n_heads, _ = q.shape
    e_heads = n_heads // kvh
    n_pages = page_table.shape[1]
    page_rows = pool.shape[1]
    past = n_pages * page_rows
    half = 2 * kvh * hd
    assert page_rows % SEL_BLOCK == 0 and page_rows % CMP_STRIDE == 0 and half % V7X_LANES == 0
    pps = 4 if n_pages % 4 == 0 else 1
    n_steps = n_pages // pps
    n_sub = past // CMP_STRIDE
    n_sel_pad = -(-(past // SEL_BLOCK + 1) // V7X_LANES) * V7X_LANES
    pb = cwin.shape[1]
    kern = functools.partial(_nsa_sample_kernel, pps=pps, kvh=kvh, e_heads=e_heads, hd=hd, n_heads=n_heads, past=past,
                             window=window, cmp_len=cmp_len, n_steps=n_steps)
    page_specs = [
        pl.BlockSpec((1, page_rows, half), lambda b, ps, st, pt, r=r: (pt[b, st * pps + r], 0, ps))
        for r in range(pps)
    ]
    grid_spec = pltpu.PrefetchScalarGridSpec(
        num_scalar_prefetch=1,
        grid=(batch, 2, n_steps),
        in_specs=page_specs + [
            pl.BlockSpec((1, n_heads, hd), lambda b, ps, st, pt: (b, 0, 0)),
            pl.BlockSpec((1, 4 * kvh, hd), lambda b, ps, st, pt: (b, 0, 0)),
            pl.BlockSpec((1, 2 * kvh, hd), lambda b, ps, st, pt: (b, 0, 0)),
            pl.BlockSpec((1, n_heads, V7X_LANES), lambda b, ps, st, pt: (b, 0, 0)),
            pl.BlockSpec((1, pb, half), lambda b, ps, st, pt: (b, 0, 0)),
            pl.BlockSpec((CMP_STRIDE, half), lambda b, ps, st, pt: (0, 0)),
            pl.BlockSpec((CMP_STRIDE, half), lambda b, ps, st, pt: (0, 0)),
        ],
        out_specs=pl.BlockSpec((1, n_heads, hd), lambda b, ps, st, pt: (b, 0, 0)),
        scratch_shapes=[
            pltpu.VMEM((n_sub, half), f32),
            pltpu.VMEM((n_sub, half), f32),
            pltpu.VMEM((kvh, V7X_SUBLANES, n_sel_pad), f32),
            pltpu.VMEM((kvh, e_heads, 1), f32),
            pltpu.VMEM((kvh, e_heads, 1), f32),
            pltpu.VMEM((kvh, e_heads, hd), f32),
            pltpu.VMEM((kvh, e_heads, hd), f32),
        ],
    )
    return pl.pallas_call(
        kern,
        out_shape=jax.ShapeDtypeStruct((batch, n_heads, hd), f32),
        grid_spec=grid_spec,
        compiler_params=_cparams(("arbitrary", "arbitrary", "arbitrary")),
        name="nsa_sample",
    )(page_table, *([pool] * pps), q, kvn, wn, gate, cwin, wh, wt)


def _group_lanes(v, groups):
    e = v.shape[-1] // groups
    v = v.reshape(v.shape[:-1] + (groups, e))
    v = jnp.pad(v, [(0, 0)] * (v.ndim - 1) + [(0, V7X_LANES - e)])
    return v.reshape(v.shape[:-2] + (groups * V7X_LANES,))


def _prep_layer(i, dims, ffn_w_up, ffn_w_down, ple_w_gate, ple_w_proj):
    ffn, ffn_pad = dims["ffn"], dims["ffn_pad"]
    out = {}
    for s in range(2):
        wu = ffn_w_up[i, s]
        pad = ((0, 0), (0, ffn_pad - ffn))
        out[f"up{s}"] = jnp.concatenate([jnp.pad(wu[:, :ffn], pad), jnp.pad(wu[:, ffn:], pad)], axis=1).astype(bf16)
        out[f"down{s}"] = jnp.pad(ffn_w_down[i, s], ((0, ffn_pad - ffn), (0, 0))).astype(bf16)
    out["ple_gate"] = ple_w_gate[i].astype(bf16)
    out["ple_proj"] = ple_w_proj[i].astype(bf16)
    return out


def _prep_ab(j, dims, ab_w_in, ab_w_out, ssm_dt_bias, ssm_a_log, ssm_d):
    ch, d_inner, xbc, groups = dims["conv_ch"], dims["d_inner"], dims["xbc"], dims["groups"]
    w = ab_w_in[j]
    o1, o2, o3 = 2 * ch, 2 * ch + d_inner, 2 * ch + d_inner + xbc
    return {
        "glu": w[:, :o1].astype(bf16),
        "z": w[:, o1:o2].astype(bf16),
        "xbc": w[:, o2:o3].astype(bf16),
        "dt": _group_lanes(w[:, o3:], groups).astype(bf16),
        "dt_bias": _group_lanes(ssm_dt_bias[j].astype(f32)[None], groups),
        "a": _group_lanes(-jnp.exp(ssm_a_log[j].astype(f32))[None], groups).reshape(groups, 1, V7X_LANES),
        "d": _group_lanes(ssm_d[j].astype(f32)[None], groups).reshape(groups, 1, V7X_LANES),
        "out_c": ab_w_out[j, :ch].astype(bf16),
        "out_y": ab_w_out[j, ch:].astype(bf16),
    }


def _prep_nsa(j, dims, nsa_w_in, nsa_w_out, nsa_cmp_w):
    qd, kvh, hd, n_heads = dims["qd"], dims["kvh"], dims["hd"], dims["n_heads"]
    e_heads = n_heads // kvh
    w = nsa_w_in[j]
    o1, o2, o3 = qd, qd + 4 * kvh * hd, qd + 6 * kvh * hd
    wg = w[:, o3:]
    wg_grp = wg.reshape(-1, 3, kvh, e_heads).transpose(0, 2, 1, 3).reshape(-1, kvh * 3 * e_heads)
    cw = nsa_cmp_w[j].astype(f32)
    cmp_len = cw.shape[1]
    cw = cw.reshape(2, cmp_len, kvh * hd)
    half = cmp_len // 2
    return {
        "q": w[:, :o1].astype(bf16),
        "kv": w[:, o1:o2].astype(bf16),
        "win": w[:, o2:o3].astype(bf16),
        "gate_grp": _group_lanes(wg_grp, kvh).astype(bf16),
        "gate_flat": jnp.pad(wg, ((0, 0), (0, V7X_LANES - wg.shape[1] % V7X_LANES))).astype(bf16),
        "out": nsa_w_out[j].astype(bf16),
        "wh": jnp.concatenate([cw[0, :half], cw[1, :half]], axis=1),
        "wt": jnp.concatenate([cw[0, half:], cw[1, half:]], axis=1),
        "cmp_len": cmp_len,
    }


def _ffn(x, g, w_up, w_down, dims):
    h = _rmsnorm(x, g)
    fp = dims["ffn_pad"]
    act = _matmul([h], [(w_up, 0), (w_up, fp)], [(0, 0, 0), (0, 1, 1)], 2, _ep_swiglu, fp, out_dtype=bf16,
                  bn_cap=512 if x.shape[0] > 64 else 2048)
    return _matmul([act], [(w_down, 0)], [(0, 0, 0)], 1, _ep_half_resid, x.shape[1], extras=[(x, "mn")],
                   nk=fp // 1024)


def _pad_rows(a, batch, rows):
    c = a.shape[1]
    return jnp.pad(a[:, None, :], ((0, 0), (0, rows - 1), (0, 0))).reshape(batch * rows, c)


def _ab_mixer(x, h, w, prm, j, st, dims, batch):
    m = x.shape[0]
    t = m // batch
    ch, d_inner, groups = dims["conv_ch"], dims["d_inner"], dims["groups"]
    u = _matmul([h], [(w["glu"], 0), (w["glu"], ch)], [(0, 0, 0), (0, 1, 1)], 2, _ep_glu, ch)
    z = _matmul([h], [(w["z"], 0)], [(0, 0, 0)], 1, _ep_plain, d_inner)
    xbc = _matmul([h], [(w["xbc"], 0)], [(0, 0, 0)], 1, _ep_plain, dims["xbc"])
    dtg = _matmul([h], [(w["dt"], 0)], [(0, 0, 0)], 1, _ep_softplus_bias, groups * V7X_LANES,
                  extras=[(w["dt_bias"], "n")])
    conv_state, sconv_state, ssm_state = st
    if t == 1:
        rc = V7X_SUBLANES
        c_pre = _causal_conv(_pad_rows(u, batch, rc), conv_state, prm["conv_dw_w"][j], prm["conv_dw_b"][j],
                             batch=batch, silu=False)[::rc]
        xbc_c = _causal_conv(_pad_rows(xbc, batch, rc), sconv_state, prm["ssm_conv_w"][j], prm["ssm_conv_b"][j],
                             batch=batch, silu=True)[::rc]
        rq = SSM_CHUNK
        y, h_new = _ssd(_pad_rows(xbc_c, batch, rq), _pad_rows(dtg, batch, rq), _pad_rows(z, batch, rq),
                        w["a"], w["d"], prm["ssm_norm_g"][j], ssm_state.astype(f32), batch=batch, groups=groups)
        y = y[::rq]
    else:
        c_pre = _causal_conv(u, conv_state, prm["conv_dw_w"][j], prm["conv_dw_b"][j], batch=batch, silu=False)
        xbc_c = _causal_conv(xbc, sconv_state, prm["ssm_conv_w"][j], prm["ssm_conv_b"][j], batch=batch, silu=True)
        y, h_new = _ssd(xbc_c, dtg, z, w["a"], w["d"], prm["ssm_norm_g"][j], ssm_state.astype(f32), batch=batch,
                        groups=groups)
    c = _ln_silu(c_pre, prm["conv_ln_g"][j], prm["conv_ln_b"][j])
    x = _matmul([c, y], [(w["out_c"], 0), (w["out_y"], 0)], [(0, 0, 0), (1, 1, 0)], 1, _ep_resid, x.shape[1],
                extras=[(x, "mn")])
    k1 = conv_state.shape[1]
    k2 = sconv_state.shape[1]
    u_ext = jnp.concatenate([conv_state.astype(f32), u.reshape(batch, t, -1)], axis=1)[:, -k1:]
    xbc_ext = jnp.concatenate([sconv_state.astype(f32), xbc.reshape(batch, t, -1)], axis=1)[:, -k2:]
    return x, u_ext, xbc_ext, h_new


def _nsa_mixer(x, h, w, past, dims, batch):
    m = x.shape[0]
    t = m // batch
    qd, kvh, hd, n_heads, window = dims["qd"], dims["kvh"], dims["hd"], dims["n_heads"], dims["window"]
    kv = _matmul([h], [(w["kv"], 0)], [(0, 0, 0)], 1, _ep_plain, 4 * kvh * hd)
    win = _matmul([h], [(w["win"], 0)], [(0, 0, 0)], 1, _ep_plain, 2 * kvh * hd)
    if past is None:
        q = _matmul([h], [(w["q"], 0)], [(0, 0, 0)], 1, _ep_plain, qd, out_dtype=bf16)
        gate = _matmul([h], [(w["gate_grp"], 0)], [(0, 0, 0)], 1, _ep_sigmoid, kvh * V7X_LANES)
        cmp = _compress(kv, w["wh"], w["wt"], batch=batch)
        o = _nsa_prompt(q, kv.astype(bf16), win.astype(bf16), cmp, gate, batch=batch, kvh=kvh, hd=hd,
                        window=window, cmp_len=w["cmp_len"])
        new_win = win.reshape(batch, t, 2, kvh, hd)[:, -window:]
    else:
        assert t == 1
        pool, cwin, page_table = past
        q = _matmul([h], [(w["q"], 0)], [(0, 0, 0)], 1, _ep_plain, qd)
        gate = _matmul([h], [(w["gate_flat"], 0)], [(0, 0, 0)], 1, _ep_sigmoid, w["gate_flat"].shape[1])
        gate = gate[:, :3 * n_heads].reshape(batch, 3, n_heads).transpose(0, 2, 1)
        gate = jnp.pad(gate, ((0, 0), (0, 0), (0, V7X_LANES - 3)))
        o = _nsa_sample(q.reshape(batch, n_heads, hd), kv.reshape(batch, 4 * kvh, hd), win.reshape(batch, 2 * kvh, hd),
                        gate, pool.reshape(pool.shape[0], pool.shape[1], -1), cwin.reshape(batch, cwin.shape[1], -1),
                        page_table, w["wh"], w["wt"], kvh=kvh, hd=hd, window=window, cmp_len=w["cmp_len"])
        o = o.reshape(batch, qd).astype(bf16)
        new_win = jnp.concatenate([cwin.astype(f32), win.reshape(batch, t, 2, kvh, hd)], axis=1)
        new_win = new_win[:, -min(window, new_win.shape[1]):]
    x = _matmul([o], [(w["out"], 0)], [(0, 0, 0)], 1, _ep_resid, x.shape[1], extras=[(x, "mn")])
    return x, kv.reshape(batch, t, 4, kvh, hd), new_win


def _trunk(x, p, prm, layers, ab, nsa, states, pasts, dims, batch):
    depth = len(layers)
    kv_rows, win_new, conv_new, sconv_new, ssm_new = [], [], [], [], []
    for i in range(depth):
        j = i // 2
        g = prm["norm_g"][i]
        lw = layers[i]
        x = _ffn(x, g[0], lw["up0"], lw["down0"], dims)
        h = _rmsnorm(x, g[1])
        if i % 2 == 0:
            x, cb, scb, sh = _ab_mixer(x, h, ab[j], prm, j, states[j], dims, batch)
            conv_new.append(cb)
            sconv_new.append(scb)
            ssm_new.append(sh)
        else:
            x, rows, wb = _nsa_mixer(x, h, nsa[j], pasts[j], dims, batch)
            kv_rows.append(rows)
            win_new.append(wb)
        x = _ffn(x, g[2], lw["up1"], lw["down1"], dims)
        hn = _rmsnorm(x, g[3])
        pp = _matmul([p[i].astype(bf16)], [(lw["ple_proj"], 0)], [(0, 0, 0)], 1, _ep_plain, x.shape[1])
        x = _matmul([hn], [(lw["ple_gate"], 0)], [(0, 0, 0)], 1, _ep_ple, x.shape[1],
                    extras=[(x, "mn"), (pp, "mn")])
    y = _rmsnorm(x, prm["final_norm_g"], out_dtype=f32)
    return (y, jnp.stack(kv_rows), jnp.stack(win_new), jnp.stack(conv_new), jnp.stack(sconv_new),
            jnp.stack(ssm_new))


def kernel(x_prompt, x_sample, cache_nsa_kv, cache_nsa_win, state_conv, state_ssm_conv, state_ssm, page_table,
           p_prompt, p_sample, norm_g, final_norm_g, ffn_w_up, ffn_w_down, ple_w_gate, ple_w_proj, ab_w_in,
           ab_w_out, conv_dw_w, conv_dw_b, conv_ln_g, conv_ln_b, ssm_conv_w, ssm_conv_b, ssm_dt_bias, ssm_a_log,
           ssm_d, ssm_norm_g, nsa_w_in, nsa_w_out, nsa_cmp_w):
    bp, seq, d_model = x_prompt.shape
    bs, dec_seq, _ = x_sample.shape
    depth = norm_g.shape[0]
    n_ab, _, heads, p_dim, n_state = state_ssm.shape
    n_c = cache_nsa_kv.shape[0]
    kvh, hd = cache_nsa_kv.shape[-2:]
    d_inner = heads * p_dim
    xbc = state_ssm_conv.shape[-1]
    ffn = ffn_w_down.shape[2]
    dims = {
        "ffn": ffn, "ffn_pad": -(-ffn // 1024) * 1024,
        "conv_ch": state_conv.shape[-1], "d_inner": d_inner, "xbc": xbc,
        "groups": (xbc - d_inner) // (2 * n_state),
        "qd": nsa_w_out.shape[1], "kvh": kvh, "hd": hd, "n_heads": nsa_w_out.shape[1] // hd,
        "window": cache_nsa_win.shape[2],
    }
    prm = {
        "norm_g": norm_g, "final_norm_g": final_norm_g, "conv_dw_w": conv_dw_w, "conv_dw_b": conv_dw_b,
        "conv_ln_g": conv_ln_g, "conv_ln_b": conv_ln_b, "ssm_conv_w": ssm_conv_w, "ssm_conv_b": ssm_conv_b,
        "ssm_norm_g": ssm_norm_g,
    }
    layers = [_prep_layer(i, dims, ffn_w_up, ffn_w_down, ple_w_gate, ple_w_proj) for i in range(depth)]
    ab = [_prep_ab(j, dims, ab_w_in, ab_w_out, ssm_dt_bias, ssm_a_log, ssm_d) for j in range(n_ab)]
    nsa = [_prep_nsa(j, dims, nsa_w_in, nsa_w_out, nsa_cmp_w) for j in range(n_c)]

    zero_states = [
        (jnp.zeros((bp,) + state_conv.shape[2:], f32), jnp.zeros((bp,) + state_ssm_conv.shape[2:], f32),
         jnp.zeros((bp, heads, p_dim, n_state), f32))
        for _ in range(n_ab)
    ]
    y_p, kv_p, win_p, conv_p, sconv_p, ssm_p = _trunk(
        x_prompt.reshape(bp * seq, d_model), p_prompt.reshape(depth, bp * seq, -1), prm, layers, ab, nsa,
        zero_states, [None] * n_c, dims, bp)

    assert dec_seq == 1
    states = [(state_conv[j], state_ssm_conv[j], state_ssm[j]) for j in range(n_ab)]
    pasts = [(cache_nsa_kv[j], cache_nsa_win[j], page_table) for j in range(n_c)]
    y_s, kv_s, win_s, conv_s, sconv_s, ssm_s = _trunk(
        x_sample.reshape(bs * dec_seq, d_model), p_sample.reshape(depth, bs * dec_seq, -1), prm, layers, ab, nsa,
        states, pasts, dims, bs)

    return (y_p.reshape(bp, seq, d_model), y_s.reshape(bs, dec_seq, d_model), kv_p, kv_s, win_p, win_s,
            conv_p, conv_s, sconv_p, sconv_s, ssm_p, ssm_s)
```

```python
import functools
from typing import NamedTuple

import jax
import jax.numpy as jnp
from jax import lax
from jax.experimental import pallas as pl
from jax.experimental.pallas import tpu as pltpu

f32 = jnp.float32
bf16 = jnp.bfloat16

V7X_LANES = 128
V7X_SUBLANES = 8
V7X_VMEM_BYTES = 64 * 1024 * 1024
VMEM_BUDGET = 40 * 1024 * 1024
VMEM_LIMIT = 56 * 1024 * 1024

NORM_EPS = 1e-6
CMP_STRIDE = 16
SEL_BLOCK = 64
SEL_TOPK = 16
SSM_CHUNK = 128
NEG = -1e30
FORCE_SCORE = 1e6


def _cparams(sem):
    return pltpu.CompilerParams(dimension_semantics=sem, vmem_limit_bytes=VMEM_LIMIT)


def _sigmoid(x):
    return jax.nn.sigmoid(x)


def _softplus(x):
    return jnp.maximum(x, 0.0) + jnp.log(1.0 + jnp.exp(-jnp.abs(x)))


def _mm_kernel(*refs, n_x, n_w, n_extra, n_acc, terms, epilogue, nk):
    xs = refs[:n_x]
    ws = refs[n_x:n_x + n_w]
    extras = refs[n_x + n_w:n_x + n_w + n_extra]
    out = refs[n_x + n_w + n_extra]
    accs = refs[n_x + n_w + n_extra + 1:]

    def partials():
        res = [None] * n_acc
        for xi, wi, ai in terms:
            d = jnp.dot(xs[xi][...], ws[wi][...], preferred_element_type=f32)
            res[ai] = d if res[ai] is None else res[ai] + d
        return res

    if nk == 1:
        out[...] = epilogue(partials(), [e[...] for e in extras]).astype(out.dtype)
        return

    k = pl.program_id(2)
    parts = partials()

    @pl.when(k == 0)
    def _():
        for a, p in zip(accs, parts):
            a[...] = p

    @pl.when(jnp.logical_and(k > 0, k < nk - 1))
    def _():
        for a, p in zip(accs, parts):
            a[...] += p

    @pl.when(k == nk - 1)
    def _():
        full = [a[...] + p for a, p in zip(accs, parts)]
        out[...] = epilogue(full, [e[...] for e in extras]).astype(out.dtype)


class _X(NamedTuple):
    arr: jax.Array
    width: int | None = None
    col_off: int = 0


class _W(NamedTuple):
    arr: jax.Array
    lead: tuple = ()
    rows: int | None = None
    row_off: int = 0
    col_off: int = 0


def _matmul(xs, ws, terms, n_acc, epilogue, n_cols, *, extras=(), out_dtype=f32, nk=1, bn_cap=1024):
    xs = [x if isinstance(x, _X) else _X(x) for x in xs]
    m = xs[0].arr.shape[0]
    bm = m if m <= 1024 else 1024
    assert m % bm == 0
    xk = [(x.width or x.arr.shape[1]) // nk for x in xs]
    wk = [(w.rows or w.arr.shape[-2]) // nk for w in ws]
    xbytes = sum(bm * k * x.arr.dtype.itemsize for x, k in zip(xs, xk))

    def est(bn):
        wbytes = sum(k * bn * w.arr.dtype.itemsize for w, k in zip(ws, wk))
        ebytes = sum((bm if kind == "mn" else 1) * bn * e.dtype.itemsize for e, kind in extras)
        obytes = bm * bn * jnp.dtype(out_dtype).itemsize
        abytes = n_acc * bm * bn * 4 * (2 if nk > 1 else 1)
        return 2 * (xbytes + wbytes + ebytes + obytes) + abytes

    cands = [c for c in (2048, 1024, 512, 256, 128)
             if c <= bn_cap and n_cols % c == 0 and all(w.col_off % c == 0 for w in ws)]
    bn = next((c for c in cands if est(c) <= VMEM_BUDGET), cands[-1])
    grid = (m // bm, n_cols // bn, nk)

    in_specs = []
    for x, bk in zip(xs, xk):
        assert x.col_off % bk == 0
        in_specs.append(pl.BlockSpec((bm, bk), lambda i, j, k, o=x.col_off // bk: (i, o + k)))
    for w, bk in zip(ws, wk):
        assert w.row_off % bk == 0
        in_specs.append(pl.BlockSpec(
            (None,) * len(w.lead) + (bk, bn),
            lambda i, j, k, lead=tuple(w.lead), ro=w.row_off // bk, co=w.col_off // bn: lead + (ro + k, co + j)))
    for e, kind in extras:
        if kind == "mn":
            in_specs.append(pl.BlockSpec((bm, bn), lambda i, j, k: (i, j)))
        else:
            in_specs.append(pl.BlockSpec((1, bn), lambda i, j, k: (0, j)))
    scratch = [pltpu.VMEM((bm, bn), f32) for _ in range(n_acc)] if nk > 1 else []
    kern = functools.partial(_mm_kernel, n_x=len(xs), n_w=len(ws), n_extra=len(extras), n_acc=n_acc,
                             terms=tuple(terms), epilogue=epilogue, nk=nk)
    return pl.pallas_call(
        kern,
        out_shape=jax.ShapeDtypeStruct((m, n_cols), out_dtype),
        grid=grid,
        in_specs=in_specs,
        out_specs=pl.BlockSpec((bm, bn), lambda i, j, k: (i, j)),
        scratch_shapes=scratch,
        compiler_params=_cparams(("parallel", "parallel", "arbitrary")),
        name="mm",
    )(*[x.arr for x in xs], *[w.arr for w in ws], *[e for e, _ in extras])


def _ep_plain(accs, extras):
    return accs[0]


def _ep_swiglu(accs, extras):
    a, b = accs
    return a * _sigmoid(a) * b


def _ep_glu(accs, extras):
    return accs[0] * _sigmoid(accs[1])


def _ep_sigmoid(accs, extras):
    return _sigmoid(accs[0])


def _ep_softplus_bias(accs, extras):
    return _softplus(accs[0] + extras[0])


def _ep_half_resid(accs, extras):
    return extras[0] + 0.5 * accs[0]


def _ep_resid(accs, extras):
    return extras[0] + accs[0]


def _ep_ple(accs, extras):
    return extras[0] + _sigmoid(accs[0]) * extras[1]


def _rmsnorm_kernel(x_ref, g_ref, o_ref):
    x = x_ref[...]
    y = x * lax.rsqrt(jnp.mean(x * x, axis=-1, keepdims=True) + NORM_EPS)
    o_ref[...] = (y * g_ref[...]).astype(o_ref.dtype)


def _ln_silu_kernel(x_ref, g_ref, b_ref, o_ref):
    x = x_ref[...]
    xc = x - jnp.mean(x, axis=-1, keepdims=True)
    var = jnp.mean(xc * xc, axis=-1, keepdims=True)
    y = xc * lax.rsqrt(var + NORM_EPS) * g_ref[...] + b_ref[...]
    o_ref[...] = (y * _sigmoid(y)).astype(o_ref.dtype)


def _rowwise(kern, x, vecs, out_dtype, name):
    m, d = x.shape
    bm = m if m <= 256 else 256
    assert m % bm == 0
    return pl.pallas_call(
        kern,
        out_shape=jax.ShapeDtypeStruct((m, d), out_dtype),
        grid=(m // bm,),
        in_specs=[pl.BlockSpec((bm, d), lambda i: (i, 0))] + [pl.BlockSpec((1, d), lambda i: (0, 0)) for _ in vecs],
        out_specs=pl.BlockSpec((bm, d), lambda i: (i, 0)),
        compiler_params=_cparams(("parallel",)),
        name=name,
    )(x, *[v.reshape(1, d).astype(f32) for v in vecs])


def _rmsnorm(x, g, out_dtype=bf16):
    return _rowwise(_rmsnorm_kernel, x, [g], out_dtype, "rmsnorm")


def _ln_silu(x, g, b):
    return _rowwise(_ln_silu_kernel, x, [g, b], bf16, "ln_silu")


def _conv_kernel(u_ref, st_ref, w_ref, b_ref, o_ref, win_ref, *shifted, taps, halo, tt, silu):
    j = pl.program_id(2)
    sub = V7X_SUBLANES

    @pl.when(j == 0)
    def _():
        win_ref[0:halo, :] = st_ref[0]

    win_ref[halo:halo + tt, :] = u_ref[...]
    off = halo - (taps - 1)
    bias = b_ref[...]
    if shifted:
        sh_ref = shifted[0]
        span = halo + tt - sub
        for s in range(1, sub):
            sh_ref[s - 1] = win_ref[s:s + span, :]

        def tap(r0, k):
            a, s = divmod(off + k, sub)
            lo = r0 + sub * a
            return win_ref[lo:lo + sub, :] if s == 0 else sh_ref[s - 1, lo:lo + sub, :]
    else:
        def tap(r0, k):
            return win_ref[r0 + off + k:r0 + off + k + sub, :]

    for r in range(tt // V7X_SUBLANES):
        r0 = r * V7X_SUBLANES
        acc = jnp.broadcast_to(bias, (V7X_SUBLANES, bias.shape[1]))
        for k in range(taps):
            acc = acc + w_ref[k:k + 1, :] * tap(r0, k)
        if silu:
            acc = acc * _sigmoid(acc)
        o_ref[r0:r0 + V7X_SUBLANES, :] = acc.astype(o_ref.dtype)
    win_ref[0:halo, :] = win_ref[tt:tt + halo, :]


def _causal_conv(u, state, w, b, *, batch, silu, out_dtype=f32):
    m, c = u.shape
    t = m // batch
    taps = w.shape[0]
    halo = -(-(taps - 1) // V7X_SUBLANES) * V7X_SUBLANES
    use_shifted = taps > V7X_SUBLANES
    tt = min(t, 256 if use_shifted else 1024)
    assert t % tt == 0 and tt % V7X_SUBLANES == 0
    cc = 512 if c % 512 == 0 else V7X_LANES
    st = jnp.pad(state.astype(f32), ((0, 0), (halo - (taps - 1), 0), (0, 0)))
    nt = t // tt
    kern = functools.partial(_conv_kernel, taps=taps, halo=halo, tt=tt, silu=silu)
    scratch = [pltpu.VMEM((halo + tt, cc), f32)]
    if use_shifted:
        scratch.append(pltpu.VMEM((V7X_SUBLANES - 1, halo + tt - V7X_SUBLANES, cc), f32))
    return pl.pallas_call(
        kern,
        out_shape=jax.ShapeDtypeStruct((m, c), out_dtype),
        grid=(batch, c // cc, nt),
        in_specs=[
            pl.BlockSpec((tt, cc), lambda bi, ci, j: (bi * nt + j, ci)),
            pl.BlockSpec((1, halo, cc), lambda bi, ci, j: (bi, 0, ci)),
            pl.BlockSpec((taps, cc), lambda bi, ci, j: (0, ci)),
            pl.BlockSpec((1, cc), lambda bi, ci, j: (0, ci)),
        ],
        out_specs=pl.BlockSpec((tt, cc), lambda bi, ci, j: (bi * nt + j, ci)),
        scratch_shapes=scratch,
        compiler_params=_cparams(("parallel", "parallel", "arbitrary")),
        name="causal_conv",
    )(u, st, w.astype(f32), b.reshape(1, c).astype(f32))


def _split3(x):
    hi = x.astype(bf16)
    r1 = x - hi.astype(f32)
    mid = r1.astype(bf16)
    lo = (r1 - mid.astype(f32)).astype(bf16)
    return hi, mid, lo


def _dot_nt(a, b):
    return lax.dot_general(a, b, (((1,), (1,)), ((), ())), preferred_element_type=f32)


def _dot_tn(a, b):
    return lax.dot_general(a, b, (((0,), (0,)), ((), ())), preferred_element_type=f32)


def _ssd_kernel(xs_ref, b_ref, c_ref, dt_ref, z_ref, a_ref, d_ref, ng_ref, h0_ref, y_ref, hout_ref, h_scr,
                *, q, e_heads, p_dim, nc):
    c = pl.program_id(2)

    @pl.when(c == 0)
    def _():
        h_scr[...] = h0_ref[0]

    xs = xs_ref[...]
    bb = b_ref[...].astype(bf16)
    cb16 = c_ref[...].astype(bf16)
    dt = dt_ref[...]
    la = dt * a_ref[0]
    ri = lax.broadcasted_iota(jnp.int32, (q, q), 0)
    ci = lax.broadcasted_iota(jnp.int32, (q, q), 1)
    causal = ri >= ci
    tri = jnp.where(causal, 1.0, 0.0).astype(bf16)
    hi, mid, lo = _split3(la)
    cs = (jnp.dot(tri, hi, preferred_element_type=f32) + jnp.dot(tri, mid, preferred_element_type=f32)
          + jnp.dot(tri, lo, preferred_element_type=f32))
    cs_t = cs.T
    cb = _dot_nt(cb16, bb)
    dsk = d_ref[0]
    ys = []
    for e in range(e_heads):
        col = cs[:, e:e + 1]
        row = cs_t[e:e + 1, :]
        decay = jnp.exp(jnp.where(causal, col - row, NEG))
        x_e = xs[:, e * p_dim:(e + 1) * p_dim]
        xd = x_e * dt[:, e:e + 1]
        y = jnp.dot((cb * decay).astype(bf16), xd.astype(bf16), preferred_element_type=f32)
        last = cs[q - 1:q, e:e + 1]
        xw = (xd * jnp.exp(last - col)).astype(bf16)
        s_new = _dot_tn(xw, bb)
        h = h_scr[e]
        y = y + _dot_nt(cb16, h.astype(bf16)) * jnp.exp(col)
        h_scr[e] = h * jnp.exp(last) + s_new
        ys.append(y + dsk[:, e:e + 1] * x_e)
    y = jnp.concatenate(ys, axis=1)
    z = z_ref[...]
    yg = y * (z * _sigmoid(z))
    yg = yg * lax.rsqrt(jnp.mean(yg * yg, axis=-1, keepdims=True) + NORM_EPS)
    y_ref[...] = (yg * ng_ref[...]).astype(y_ref.dtype)

    @pl.when(c == nc - 1)
    def _():
        hout_ref[0] = h_scr[...]


def _ssd(xbc, dtg, z, a_g, d_g, norm_g, h0, *, batch, groups):
    m = xbc.shape[0]
    t = m // batch
    _, heads, p_dim, n = h0.shape
    d_inner = heads * p_dim
    e_heads = heads // groups
    gw = e_heads * p_dim
    q = SSM_CHUNK
    assert t % q == 0 and gw % V7X_LANES == 0 and n % V7X_LANES == 0
    nc = t // q
    b_off = d_inner // n
    c_off = b_off + groups
    kern = functools.partial(_ssd_kernel, q=q, e_heads=e_heads, p_dim=p_dim, nc=nc)
    return pl.pallas_call(
        kern,
        out_shape=(jax.ShapeDtypeStruct((m, d_inner), bf16), jax.ShapeDtypeStruct((batch, heads, p_dim, n), f32)),
        grid=(batch, groups, nc),
        in_specs=[
            pl.BlockSpec((q, gw), lambda bi, g, c: (bi * nc + c, g)),
            pl.BlockSpec((q, n), lambda bi, g, c: (bi * nc + c, b_off + g)),
            pl.BlockSpec((q, n), lambda bi, g, c: (bi * nc + c, c_off + g)),
            pl.BlockSpec((q, V7X_LANES), lambda bi, g, c: (bi * nc + c, g)),
            pl.BlockSpec((q, gw), lambda bi, g, c: (bi * nc + c, g)),
            pl.BlockSpec((1, 1, V7X_LANES), lambda bi, g, c: (g, 0, 0)),
            pl.BlockSpec((1, 1, V7X_LANES), lambda bi, g, c: (g, 0, 0)),
            pl.BlockSpec((1, gw), lambda bi, g, c: (0, g)),
            pl.BlockSpec((1, e_heads, p_dim, n), lambda bi, g, c: (bi, g, 0, 0)),
        ],
        out_specs=(
            pl.BlockSpec((q, gw), lambda bi, g, c: (bi * nc + c, g)),
            pl.BlockSpec((1, e_heads, p_dim, n), lambda bi, g, c: (bi, g, 0, 0)),
        ),
        scratch_shapes=[pltpu.VMEM((e_heads, p_dim, n), f32)],
        compiler_params=_cparams(("parallel", "parallel", "arbitrary")),
        name="ssd",
    )(xbc, xbc, xbc, dtg, z, a_g, d_g, norm_g.reshape(1, d_inner).astype(f32), h0)


def _log2(v):
    assert v > 0 and v & (v - 1) == 0, v
    return v.bit_length() - 1


def _masked_softmax(s, mask):
    s = jnp.where(mask, s, NEG)
    m = jnp.max(s, axis=-1, keepdims=True)
    e = jnp.where(mask, jnp.exp(s - m), 0.0)
    return e / jnp.maximum(jnp.sum(e, axis=-1, keepdims=True), 1e-30)


def _block_sums(imp, n_lanes):
    n = imp.shape[1]
    shift = _log2(SEL_BLOCK // CMP_STRIDE)
    ii = lax.broadcasted_iota(jnp.int32, (n, n_lanes), 0)
    jj = lax.broadcasted_iota(jnp.int32, (n, n_lanes), 1)
    gm = jnp.where(lax.shift_right_logical(ii, shift) == jj, 1.0, 0.0).astype(bf16)
    hi, mid, lo = _split3(imp)
    return (jnp.dot(hi, gm, preferred_element_type=f32) + jnp.dot(mid, gm, preferred_element_type=f32)
            + jnp.dot(lo, gm, preferred_element_type=f32))


def _compress_partial(rows, wh, wt, stride):
    t, c = rows.shape
    r = rows.reshape(t // stride, stride, c)
    return jnp.sum(r * wh[None], axis=1), jnp.sum(r * wt[None], axis=1)


def _compress_combine(head, tail):
    n, c = head.shape
    nxt = pltpu.roll(tail, n - 1, 0)
    keep = lax.broadcasted_iota(jnp.int32, (n, c), 0) < n - 1
    return head + jnp.where(keep, nxt, 0.0)


def _compress_kernel(rows_ref, wh_ref, wt_ref, o_ref, *, stride):
    head, tail = _compress_partial(rows_ref[...], wh_ref[...], wt_ref[...], stride)
    o_ref[0] = _compress_combine(head, tail)


def _compress(kv, wh, wt, *, batch):
    m = kv.shape[0]
    t = m // batch
    stride, c2 = wh.shape
    cc = 256
    assert c2 % cc == 0 and t % stride == 0
    return pl.pallas_call(
        functools.partial(_compress_kernel, stride=stride),
        out_shape=jax.ShapeDtypeStruct((batch, t // stride, c2), f32),
        grid=(batch, c2 // cc),
        in_specs=[
            pl.BlockSpec((t, cc), lambda bi, ci: (bi, ci)),
            pl.BlockSpec((stride, cc), lambda bi, ci: (0, ci)),
            pl.BlockSpec((stride, cc), lambda bi, ci: (0, ci)),
        ],
        out_specs=pl.BlockSpec((1, t // stride, cc), lambda bi, ci: (bi, 0, ci)),
        compiler_params=_cparams(("parallel", "parallel")),
        name="nsa_compress",
    )(kv, wh, wt)


_LOG2E = 1.4426950408889634


def _nsa_prompt_kernel(q_ref, ks_ref, vs_ref, kw_ref, vw_ref, kc_ref, vc_ref, gate_ref, o_ref, tm_scr,
                       *, tq, e_heads, hd, n_heads, t, window, cmp_len, tk_sel):
    g = pl.program_id(1)
    i = pl.program_id(2)
    q0 = i * tq
    rows = e_heads * tq
    scale = hd ** -0.5
    q = q_ref[...]
    qs = jnp.concatenate([q[:, e * hd:(e + 1) * hd] for e in range(e_heads)], axis=0)
    r_idx = lax.broadcasted_iota(jnp.int32, (rows, 1), 0)
    e_idx = lax.shift_right_logical(r_idx, _log2(tq))
    t_idx = r_idx - e_idx * tq
    slope = jnp.exp2(-8.0 * (g * e_heads + e_idx + 1).astype(f32) / n_heads)
    qpos = q0 + t_idx

    nc = kc_ref.shape[1]
    kc = kc_ref[0].astype(bf16)
    vc = vc_ref[0].astype(bf16)
    s = _dot_nt(qs, kc) * scale
    c_end = lax.broadcasted_iota(jnp.int32, (1, nc), 1) * CMP_STRIDE + (cmp_len - 1)
    dist = qpos - c_end
    s = s - slope * dist.astype(f32)
    p_c = _masked_softmax(s, dist >= 0)
    o_c = jnp.dot(p_c.astype(bf16), vc, preferred_element_type=f32)
    imp = jnp.sum(p_c.reshape(e_heads, tq, nc), axis=0)

    n_sel = -(-t // SEL_BLOCK)
    k_top = min(SEL_TOPK, n_sel)
    nsp = -(-n_sel // V7X_SUBLANES) * V7X_SUBLANES
    shift = _log2(SEL_BLOCK // CMP_STRIDE)
    gj = lax.broadcasted_iota(jnp.int32, (nsp, nc), 0)
    gi = lax.shift_right_logical(lax.broadcasted_iota(jnp.int32, (nsp, nc), 1), shift)
    gm_t = jnp.where(gi == gj, 1.0, 0.0).astype(bf16)
    hi, mid, lo = _split3(imp)
    bimp_t = _dot_nt(gm_t, hi) + _dot_nt(gm_t, mid) + _dot_nt(gm_t, lo)
    jl = lax.broadcasted_iota(jnp.int32, (nsp, tq), 0)
    tpos = q0 + lax.broadcasted_iota(jnp.int32, (nsp, tq), 1)
    cur = lax.shift_right_logical(tpos, _log2(SEL_BLOCK))
    valid = jl * SEL_BLOCK <= tpos
    forced = (jl == 0) | (jl == cur) | (jl == cur - 1)
    score = jnp.where(valid, jnp.where(forced, FORCE_SCORE, bimp_t), -FORCE_SCORE)
    rank = jnp.zeros((nsp, tq), f32)
    for j2 in range(n_sel):
        other = score[j2:j2 + 1, :]
        beats = (other > score) | ((other == score) & (jl > j2))
        rank = rank + jnp.where(beats, 1.0, 0.0)
    sel_t = jnp.where(rank < k_top, 1.0, 0.0).astype(bf16)

    wk = tm_scr.shape[1]
    rel = (lax.broadcasted_iota(jnp.int32, (rows, wk), 1) - t_idx).astype(f32)
    tm_scr[...] = slope * (rel * _LOG2E)
    c_scale = scale * _LOG2E

    def sel_body(kt, carry):
        m, l, acc = carry
        k0 = pl.multiple_of(kt * tk_sel, tk_sel)
        kb = ks_ref[pl.ds(k0, tk_sel), :]
        vb = vs_ref[pl.ds(k0, tk_sel), :]
        s2 = _dot_nt(qs, kb) * c_scale + tm_scr[:, 0:tk_sel]
        jj = lax.broadcasted_iota(jnp.int32, (nsp, tk_sel), 0)
        kk = lax.shift_right_logical(k0 + lax.broadcasted_iota(jnp.int32, (nsp, tk_sel), 1), _log2(SEL_BLOCK))
        expand = jnp.where(jj == kk, 1.0, 0.0).astype(bf16)
        ms = _dot_tn(sel_t, expand)
        tt_ = q0 + lax.broadcasted_iota(jnp.int32, (tq, tk_sel), 0)
        cc_ = k0 + lax.broadcasted_iota(jnp.int32, (tq, tk_sel), 1)
        madd = jnp.where((ms > 0.5) & (cc_ <= tt_), 0.0, NEG)
        s3 = (s2.reshape(e_heads, tq, tk_sel) + madd[None]).reshape(rows, tk_sel)
        rowoff = slope * ((k0 - q0).astype(f32) * _LOG2E)
        m_new = jnp.maximum(m, jnp.max(s3, axis=-1, keepdims=True) + rowoff)
        p = jnp.exp2(s3 - (m_new - rowoff))
        alpha = jnp.exp2(m - m_new)
        l = alpha * l + jnp.sum(p, axis=-1, keepdims=True)
        acc = alpha * acc + jnp.dot(p.astype(bf16), vb, preferred_element_type=f32)
        return m_new, l, acc

    init = (jnp.full((rows, 1), NEG, f32), jnp.zeros((rows, 1), f32), jnp.zeros((rows, hd), f32))
    hi_sel = (q0 + tq + tk_sel - 1) // tk_sel
    _, l_s, acc_s = lax.fori_loop(0, hi_sel, sel_body, init)
    o_s = acc_s / jnp.maximum(l_s, 1e-30)

    start = pl.multiple_of(jnp.clip(q0 - window, 0, t - wk), tq)
    kb = kw_ref[pl.ds(start, wk), :]
    vb = vw_ref[pl.ds(start, wk), :]
    s2 = _dot_nt(qs, kb) * c_scale + tm_scr[...]
    dw = (q0 + lax.broadcasted_iota(jnp.int32, (tq, wk), 0)) - (start + lax.broadcasted_iota(jnp.int32, (tq, wk), 1))
    madd = jnp.where((dw >= 0) & (dw < window), 0.0, NEG)
    s3 = (s2.reshape(e_heads, tq, wk) + madd[None]).reshape(rows, wk)
    p = jnp.exp2(s3 - jnp.max(s3, axis=-1, keepdims=True))
    o_w = (jnp.dot(p.astype(bf16), vb, preferred_element_type=f32)
           / jnp.maximum(jnp.sum(p, axis=-1, keepdims=True), 1e-30))

    gates = gate_ref[...]
    outs = []
    for e in range(e_heads):
        sl = slice(e * tq, (e + 1) * tq)
        outs.append(gates[:, e:e + 1] * o_c[sl] + gates[:, e_heads + e:e_heads + e + 1] * o_s[sl]
                    + gates[:, 2 * e_heads + e:2 * e_heads + e + 1] * o_w[sl])
    o_ref[...] = jnp.concatenate(outs, axis=1).astype(o_ref.dtype)


def _nsa_prompt(q, kvb, winb, cmp, gate, *, batch, kvh, hd, window, cmp_len):
    m, qd = q.shape
    t = m // batch
    n_heads = qd // hd
    e_heads = n_heads // kvh
    tq = 128
    tk_sel = 256
    wk = window + tq
    assert t % tq == 0 and t % tk_sel == 0 and window % tq == 0 and t >= wk and wk >= tk_sel
    nq = t // tq
    nc = cmp.shape[1]
    gw = e_heads * hd
    kern = functools.partial(_nsa_prompt_kernel, tq=tq, e_heads=e_heads, hd=hd, n_heads=n_heads, t=t, window=window,
                             cmp_len=cmp_len, tk_sel=tk_sel)
    return pl.pallas_call(
        kern,
        out_shape=jax.ShapeDtypeStruct((m, qd), bf16),
        grid=(batch, kvh, nq),
        in_specs=[
            pl.BlockSpec((tq, gw), lambda bi, g, i: (bi * nq + i, g)),
            pl.BlockSpec((t, hd), lambda bi, g, i: (bi, 2 * kvh + g)),
            pl.BlockSpec((t, hd), lambda bi, g, i: (bi, 3 * kvh + g)),
            pl.BlockSpec((t, hd), lambda bi, g, i: (bi, g)),
            pl.BlockSpec((t, hd), lambda bi, g, i: (bi, kvh + g)),
            pl.BlockSpec((1, nc, hd), lambda bi, g, i: (bi, 0, g)),
            pl.BlockSpec((1, nc, hd), lambda bi, g, i: (bi, 0, kvh + g)),
            pl.BlockSpec((tq, V7X_LANES), lambda bi, g, i: (bi * nq + i, g)),
        ],
        out_specs=pl.BlockSpec((tq, gw), lambda bi, g, i: (bi * nq + i, g)),
        scratch_shapes=[pltpu.VMEM((e_heads * tq, wk), f32)],
        compiler_params=_cparams(("parallel", "parallel", "parallel")),
        name="nsa_prompt",
    )(q, kvb, kvb, winb, winb, cmp, cmp, gate)


def _nsa_sample_kernel(pt_ref, *refs, pps, kvh, e_heads, hd, n_heads, past, window, cmp_len, n_steps):
    page_refs = refs[:pps]
    q_ref, kvn_ref, wn_ref, gate_ref, cwin_ref, wh_ref, wt_ref, o_ref = refs[pps:pps + 8]
    head_scr, tail_scr, sel_scr, m_scr, l_scr, acc_scr, oc_scr = refs[pps + 8:]
    ps = pl.program_id(1)
    st = pl.program_id(2)
    scale = hd ** -0.5
    page_rows = page_refs[0].shape[2]
    sub = page_rows // CMP_STRIDE
    n_sub = past // CMP_STRIDE
    n_sel = past // SEL_BLOCK + 1
    k_top = min(SEL_TOPK, n_sel)
    n_sel_pad = sel_scr.shape[2]
    cur = past // SEL_BLOCK

    def slope_of(g):
        e_idx = lax.broadcasted_iota(jnp.int32, (e_heads, 1), 0)
        return jnp.exp2(-8.0 * (g * e_heads + e_idx + 1).astype(f32) / n_heads)

    def q_of(g):
        return q_ref[0, g * e_heads:(g + 1) * e_heads, :].astype(bf16)

    def rnd(x):
        return x.astype(bf16).astype(f32)

    @pl.when(ps == 0)
    def _():
        for r in range(pps):
            row0 = pl.multiple_of((st * pps + r) * sub, sub)
            for c in range(2 * kvh):
                cols = slice(c * hd, (c + 1) * hd)
                head, tail = _compress_partial(page_refs[r][0, 0, :, c, :], wh_ref[:, cols], wt_ref[:, cols],
                                               CMP_STRIDE)
                head_scr[pl.ds(row0, sub), cols] = head
                tail_scr[pl.ds(row0, sub), cols] = tail

    @pl.when(jnp.logical_and(ps == 1, st == 0))
    def _():
        cmp = _compress_combine(head_scr[...], tail_scr[...])
        c_end = lax.broadcasted_iota(jnp.int32, (1, n_sub), 1) * CMP_STRIDE + (cmp_len - 1)
        dist = past - c_end
        jl = lax.broadcasted_iota(jnp.int32, (V7X_SUBLANES, n_sel_pad), 1)
        valid = jl * SEL_BLOCK <= past
        forced = (jl == 0) | (jl == cur) | (jl == cur - 1)
        j2 = lax.broadcasted_iota(jnp.int32, (n_sel_pad, n_sel_pad), 0)
        j1 = lax.broadcasted_iota(jnp.int32, (n_sel_pad, n_sel_pad), 1)
        for g in range(kvh):
            kc = cmp[:, g * hd:(g + 1) * hd].astype(bf16)
            vc = cmp[:, (kvh + g) * hd:(kvh + g + 1) * hd].astype(bf16)
            s = _dot_nt(q_of(g), kc) * scale - slope_of(g) * dist.astype(f32)
            p = _masked_softmax(s, dist >= 0)
            oc_scr[g] = jnp.dot(p.astype(bf16), vc, preferred_element_type=f32)
            imp = jnp.sum(p, axis=0, keepdims=True)
            bimp = _block_sums(jnp.broadcast_to(imp, (V7X_SUBLANES, n_sub)), n_sel_pad)
            score = jnp.where(valid, jnp.where(forced, FORCE_SCORE, bimp), -FORCE_SCORE)
            score = jnp.where(jl < n_sel, score, -2.0 * FORCE_SCORE)
            sc_row = jnp.broadcast_to(score[0:1], (n_sel_pad, n_sel_pad))
            sc_col = sc_row.T
            beats = ((sc_col > sc_row) | ((sc_col == sc_row) & (j2 < j1))) & (j2 < n_sel)
            rank = jnp.sum(jnp.where(beats, 1.0, 0.0), axis=0, keepdims=True)
            sel_scr[g] = jnp.broadcast_to(jnp.where(rank < k_top, 1.0, 0.0), (V7X_SUBLANES, n_sel_pad))
        m_scr[...] = jnp.full(m_scr.shape, NEG, f32)
        l_scr[...] = jnp.zeros(l_scr.shape, f32)
        acc_scr[...] = jnp.zeros(acc_scr.shape, f32)

    @pl.when(ps == 1)
    def _():
        for r in range(pps):
            page = page_refs[r]
            k0 = (st * pps + r) * page_rows
            kpos = k0 + lax.broadcasted_iota(jnp.int32, (1, page_rows), 1)
            d = past - kpos
            jj = lax.broadcasted_iota(jnp.int32, (n_sel_pad, page_rows), 0)
            kk = lax.shift_right_logical(k0 + lax.broadcasted_iota(jnp.int32, (n_sel_pad, page_rows), 1),
                                         _log2(SEL_BLOCK))
            expand = jnp.where(jj == kk, 1.0, 0.0).astype(bf16)
            for g in range(kvh):
                kb = page[0, 0, :, g, :].astype(bf16)
                vb = page[0, 0, :, kvh + g, :].astype(bf16)
                sc = _dot_nt(q_of(g), kb) * scale - slope_of(g) * d.astype(f32)
                ms = jnp.dot(sel_scr[g].astype(bf16), expand, preferred_element_type=f32)
                mask = (ms[0:1] > 0.5) & (d >= 0)
                sc = jnp.where(mask, sc, NEG)
                m = m_scr[g]
                m_new = jnp.maximum(m, jnp.max(sc, axis=-1, keepdims=True))
                alpha = jnp.exp(m - m_new)
                p = jnp.where(mask, jnp.exp(sc - m_new), 0.0)
                l_scr[g] = alpha * l_scr[g] + jnp.sum(p, axis=-1, keepdims=True)
                acc_scr[g] = alpha * acc_scr[g] + jnp.dot(p.astype(bf16), vb, preferred_element_type=f32)
                m_scr[g] = m_new

    @pl.when(jnp.logical_and(ps == 1, st == n_steps - 1))
    def _():
        pb = cwin_ref.shape[2]
        wpos = past - pb + lax.broadcasted_iota(jnp.int32, (1, pb), 1)
        dw = past - wpos
        wmask = (dw >= 0) & (dw < window) & (wpos >= 0)
        for g in range(kvh):
            qb = q_of(g)
            q32 = qb.astype(f32)
            slope = slope_of(g)
            kn = rnd(kvn_ref[0, 2 * kvh + g:2 * kvh + g + 1, :])
            vn = rnd(kvn_ref[0, 3 * kvh + g:3 * kvh + g + 1, :])
            sel_new = sel_scr[g][0:1, n_sel - 1:n_sel] > 0.5
            s_n = jnp.where(sel_new, jnp.sum(q32 * kn, axis=-1, keepdims=True) * scale, NEG)
            m = m_scr[g]
            m_new = jnp.maximum(m, s_n)
            alpha = jnp.exp(m - m_new)
            p_n = jnp.where(sel_new, jnp.exp(s_n - m_new), 0.0)
            l = alpha * l_scr[g] + p_n
            acc = alpha * acc_scr[g] + rnd(p_n) * vn
            o_s = acc / jnp.maximum(l, 1e-30)
            kw = cwin_ref[0, 0, :, g, :].astype(bf16)
            vw = cwin_ref[0, 0, :, kvh + g, :].astype(bf16)
            sw = jnp.where(wmask, _dot_nt(qb, kw) * scale - slope * dw.astype(f32), NEG)
            kwn = rnd(wn_ref[0, g:g + 1, :])
            vwn = rnd(wn_ref[0, kvh + g:kvh + g + 1, :])
            sw_n = jnp.sum(q32 * kwn, axis=-1, keepdims=True) * scale
            mw = jnp.maximum(jnp.max(sw, axis=-1, keepdims=True), sw_n)
            ew = jnp.where(wmask, jnp.exp(sw - mw), 0.0)
            en = jnp.exp(sw_n - mw)
            den = jnp.maximum(jnp.sum(ew, axis=-1, keepdims=True) + en, 1e-30)
            o_w = (jnp.dot(ew.astype(bf16), vw, preferred_element_type=f32) + rnd(en) * vwn) / den
            gates = gate_ref[0, g * e_heads:(g + 1) * e_heads, :]
            o = gates[:, 0:1] * oc_scr[g] + gates[:, 1:2] * o_s + gates[:, 2:3] * o_w
            o_ref[0, g * e_heads:(g + 1) * e_heads, :] = o


def _nsa_sample(q, kvn, wn, gate, pool, cwin, page_table, wh, wt, *, layer, kvh, hd, window, cmp_len):
    batch, n_heads, _ = q.shape
    e_heads = n_heads // kvh
    n_pages = page_table.shape[1]
    page_rows = pool.shape[2]
    past = n_pages * page_rows
    half = 2 * kvh * hd
    assert page_rows % SEL_BLOCK == 0 and page_rows % CMP_STRIDE == 0 and (2 * kvh) % V7X_SUBLANES == 0
    pps = 4 if n_pages % 4 == 0 else 1
    n_steps = n_pages // pps
    n_sub = past // CMP_STRIDE
    n_sel_pad = -(-(past // SEL_BLOCK + 1) // V7X_LANES) * V7X_LANES
    pb = cwin.shape[2]
    kern = functools.partial(_nsa_sample_kernel, pps=pps, kvh=kvh, e_heads=e_heads, hd=hd, n_heads=n_heads, past=past,
                             window=window, cmp_len=cmp_len, n_steps=n_steps)
    page_specs = [
        pl.BlockSpec((1, 1, page_rows, 2 * kvh, hd),
                     lambda b, ps, st, pt, r=r: (layer, pt[b, st * pps + r], 0, ps, 0))
        for r in range(pps)
    ]
    grid_spec = pltpu.PrefetchScalarGridSpec(
        num_scalar_prefetch=1,
        grid=(batch, 2, n_steps),
        in_specs=page_specs + [
            pl.BlockSpec((1, n_heads, hd), lambda b, ps, st, pt: (b, 0, 0)),
            pl.BlockSpec((1, 4 * kvh, hd), lambda b, ps, st, pt: (b, 0, 0)),
            pl.BlockSpec((1, 2 * kvh, hd), lambda b, ps, st, pt: (b, 0, 0)),
            pl.BlockSpec((1, n_heads, V7X_LANES), lambda b, ps, st, pt: (b, 0, 0)),
            pl.BlockSpec((1, 1, pb, 2 * kvh, hd), lambda b, ps, st, pt: (layer, b, 0, 0, 0)),
            pl.BlockSpec((CMP_STRIDE, half), lambda b, ps, st, pt: (0, 0)),
            pl.BlockSpec((CMP_STRIDE, half), lambda b, ps, st, pt: (0, 0)),
        ],
        out_specs=pl.BlockSpec((1, n_heads, hd), lambda b, ps, st, pt: (b, 0, 0)),
        scratch_shapes=[
            pltpu.VMEM((n_sub, half), f32),
            pltpu.VMEM((n_sub, half), f32),
            pltpu.VMEM((kvh, V7X_SUBLANES, n_sel_pad), f32),
            pltpu.VMEM((kvh, e_heads, 1), f32),
            pltpu.VMEM((kvh, e_heads, 1), f32),
            pltpu.VMEM((kvh, e_heads, hd), f32),
            pltpu.VMEM((kvh, e_heads, hd), f32),
        ],
    )
    return pl.pallas_call(
        kern,
        out_shape=jax.ShapeDtypeStruct((batch, n_heads, hd), f32),
        grid_spec=grid_spec,
        compiler_params=_cparams(("arbitrary", "arbitrary", "arbitrary")),
        name="nsa_sample",
    )(page_table, *([pool] * pps), q, kvn, wn, gate, cwin, wh, wt)


def _group_lanes(v, groups):
    e = v.shape[-1] // groups
    v = v.reshape(v.shape[:-1] + (groups, e))
    v = jnp.pad(v, [(0, 0)] * (v.ndim - 1) + [(0, V7X_LANES - e)])
    return v.reshape(v.shape[:-2] + (groups * V7X_LANES,))


def _prep_dense(dims, ffn_w_up, ffn_w_down, ple_w_gate, ple_w_proj, ab_w_in, ab_w_out, nsa_w_in, nsa_w_out):
    ffn, fp = dims["ffn"], dims["ffn_pad"]
    depth, _, d_model, _ = ffn_w_up.shape
    up = jnp.pad(ffn_w_up.reshape(depth, 2, d_model, 2, ffn), ((0, 0),) * 4 + ((0, fp - ffn),))
    return {
        "up": up.astype(bf16).reshape(depth, 2, d_model, 2 * fp),
        "down": jnp.pad(ffn_w_down, ((0, 0), (0, 0), (0, fp - ffn), (0, 0))).astype(bf16),
        "ple_gate": ple_w_gate.astype(bf16),
        "ple_proj": ple_w_proj.astype(bf16),
        "ab_in": ab_w_in.astype(bf16),
        "ab_out": ab_w_out.astype(bf16),
        "nsa_in": nsa_w_in.astype(bf16),
        "nsa_out": nsa_w_out.astype(bf16),
    }


def _prep_ab(j, dims, ab_w_in, ssm_dt_bias, ssm_a_log, ssm_d):
    groups = dims["groups"]
    o3 = 2 * dims["conv_ch"] + dims["d_inner"] + dims["xbc"]
    return {
        "dt": _group_lanes(ab_w_in[j, :, o3:], groups).astype(bf16),
        "dt_bias": _group_lanes(ssm_dt_bias[j].astype(f32)[None], groups),
        "a": _group_lanes(-jnp.exp(ssm_a_log[j].astype(f32))[None], groups).reshape(groups, 1, V7X_LANES),
        "d": _group_lanes(ssm_d[j].astype(f32)[None], groups).reshape(groups, 1, V7X_LANES),
    }


def _prep_nsa(j, dims, nsa_w_in, nsa_cmp_w):
    qd, kvh, hd, n_heads = dims["qd"], dims["kvh"], dims["hd"], dims["n_heads"]
    e_heads = n_heads // kvh
    wg = nsa_w_in[j, :, qd + 6 * kvh * hd:]
    wg_grp = wg.reshape(-1, 3, kvh, e_heads).transpose(0, 2, 1, 3).reshape(-1, kvh * 3 * e_heads)
    cw = nsa_cmp_w[j].astype(f32)
    cmp_len = cw.shape[1]
    cw = cw.reshape(2, cmp_len, kvh * hd)
    half = cmp_len // 2
    return {
        "gate_grp": _group_lanes(wg_grp, kvh).astype(bf16),
        "gate_flat": jnp.pad(wg, ((0, 0), (0, V7X_LANES - wg.shape[1] % V7X_LANES))).astype(bf16),
        "wh": jnp.concatenate([cw[0, :half], cw[1, :half]], axis=1),
        "wt": jnp.concatenate([cw[0, half:], cw[1, half:]], axis=1),
        "cmp_len": cmp_len,
    }


_DOWN_BK_CAP = 2816


def _ffn(x, g, dense, i, s, dims):
    h = _rmsnorm(x, g)
    fp = dims["ffn_pad"]
    act = _matmul([h], [_W(dense["up"], (i, s)), _W(dense["up"], (i, s), col_off=fp)], [(0, 0, 0), (0, 1, 1)], 2,
                  _ep_swiglu, fp, out_dtype=bf16, bn_cap=512 if x.shape[0] > 64 else 2048)
    units = fp // V7X_LANES
    nk = min(d for d in range(1, units + 1) if units % d == 0 and fp // d <= _DOWN_BK_CAP)
    return _matmul([act], [_W(dense["down"], (i, s))], [(0, 0, 0)], 1, _ep_half_resid, x.shape[1],
                   extras=[(x, "mn")], nk=nk)


def _pad_rows(a, batch, rows):
    c = a.shape[1]
    return jnp.pad(a[:, None, :], ((0, 0), (0, rows - 1), (0, 0))).reshape(batch * rows, c)


def _ab_mixer(x, h, dense, w, prm, j, st, dims, batch):
    m = x.shape[0]
    t = m // batch
    ch, d_inner, groups = dims["conv_ch"], dims["d_inner"], dims["groups"]
    w_in = dense["ab_in"]
    u = _matmul([h], [_W(w_in, (j,)), _W(w_in, (j,), col_off=ch)], [(0, 0, 0), (0, 1, 1)], 2, _ep_glu, ch)
    z = _matmul([h], [_W(w_in, (j,), col_off=2 * ch)], [(0, 0, 0)], 1, _ep_plain, d_inner)
    xbc = _matmul([h], [_W(w_in, (j,), col_off=2 * ch + d_inner)], [(0, 0, 0)], 1, _ep_plain, dims["xbc"])
    dtg = _matmul([h], [_W(w["dt"])], [(0, 0, 0)], 1, _ep_softplus_bias, groups * V7X_LANES,
                  extras=[(w["dt_bias"], "n")])
    conv_state, sconv_state, ssm_state = st
    if t == 1:
        rc = V7X_SUBLANES
        c_pre = _causal_conv(_pad_rows(u, batch, rc), conv_state, prm["conv_dw_w"][j], prm["conv_dw_b"][j],
                             batch=batch, silu=False)[::rc]
        xbc_c = _causal_conv(_pad_rows(xbc, batch, rc), sconv_state, prm["ssm_conv_w"][j], prm["ssm_conv_b"][j],
                             batch=batch, silu=True)[::rc]
        rq = SSM_CHUNK
        y, h_new = _ssd(_pad_rows(xbc_c, batch, rq), _pad_rows(dtg, batch, rq), _pad_rows(z, batch, rq),
                        w["a"], w["d"], prm["ssm_norm_g"][j], ssm_state.astype(f32), batch=batch, groups=groups)
        y = y[::rq]
    else:
        c_pre = _causal_conv(u, conv_state, prm["conv_dw_w"][j], prm["conv_dw_b"][j], batch=batch, silu=False)
        xbc_c = _causal_conv(xbc, sconv_state, prm["ssm_conv_w"][j], prm["ssm_conv_b"][j], batch=batch, silu=True)
        y, h_new = _ssd(xbc_c, dtg, z, w["a"], w["d"], prm["ssm_norm_g"][j], ssm_state.astype(f32), batch=batch,
                        groups=groups)
    c = _ln_silu(c_pre, prm["conv_ln_g"][j], prm["conv_ln_b"][j])
    assert d_inner % ch == 0
    n_y = d_inner // ch
    xs = [c] + [_X(y, ch, r * ch) for r in range(n_y)]
    ws = [_W(dense["ab_out"], (j,), rows=ch, row_off=r * ch) for r in range(n_y + 1)]
    x = _matmul(xs, ws, [(r, r, 0) for r in range(n_y + 1)], 1, _ep_resid, x.shape[1], extras=[(x, "mn")])
    k1 = conv_state.shape[1]
    k2 = sconv_state.shape[1]
    u_ext = jnp.concatenate([conv_state.astype(f32), u.reshape(batch, t, -1)], axis=1)[:, -k1:]
    xbc_ext = jnp.concatenate([sconv_state.astype(f32), xbc.reshape(batch, t, -1)], axis=1)[:, -k2:]
    return x, u_ext, xbc_ext, h_new


def _nsa_mixer(x, h, dense, w, j, past, dims, batch):
    m = x.shape[0]
    t = m // batch
    qd, kvh, hd, n_heads, window = dims["qd"], dims["kvh"], dims["hd"], dims["n_heads"], dims["window"]
    w_in = dense["nsa_in"]
    kv = _matmul([h], [_W(w_in, (j,), col_off=qd)], [(0, 0, 0)], 1, _ep_plain, 4 * kvh * hd)
    win = _matmul([h], [_W(w_in, (j,), col_off=qd + 4 * kvh * hd)], [(0, 0, 0)], 1, _ep_plain, 2 * kvh * hd)
    if past is None:
        q = _matmul([h], [_W(w_in, (j,))], [(0, 0, 0)], 1, _ep_plain, qd, out_dtype=bf16)
        gate = _matmul([h], [_W(w["gate_grp"])], [(0, 0, 0)], 1, _ep_sigmoid, kvh * V7X_LANES)
        cmp = _compress(kv, w["wh"], w["wt"], batch=batch)
        o = _nsa_prompt(q, kv.astype(bf16), win.astype(bf16), cmp, gate, batch=batch, kvh=kvh, hd=hd,
                        window=window, cmp_len=w["cmp_len"])
        new_win = win.reshape(batch, t, 2, kvh, hd)[:, -window:]
    else:
        assert t == 1
        cache_kv, cache_win, page_table = past
        q = _matmul([h], [_W(w_in, (j,))], [(0, 0, 0)], 1, _ep_plain, qd)
        gate = _matmul([h], [_W(w["gate_flat"])], [(0, 0, 0)], 1, _ep_sigmoid, w["gate_flat"].shape[1])
        gate = gate[:, :3 * n_heads].reshape(batch, 3, n_heads).transpose(0, 2, 1)
        gate = jnp.pad(gate, ((0, 0), (0, 0), (0, V7X_LANES - 3)))
        pool = cache_kv.reshape(cache_kv.shape[:3] + (4 * kvh, hd))
        cwin = cache_win.reshape(cache_win.shape[:3] + (2 * kvh, hd))
        o = _nsa_sample(q.reshape(batch, n_heads, hd), kv.reshape(batch, 4 * kvh, hd), win.reshape(batch, 2 * kvh, hd),
                        gate, pool, cwin, page_table, w["wh"], w["wt"], layer=j, kvh=kvh, hd=hd, window=window,
                        cmp_len=w["cmp_len"])
        o = o.reshape(batch, qd).astype(bf16)
        new_win = jnp.concatenate([cache_win[j].astype(f32), win.reshape(batch, t, 2, kvh, hd)], axis=1)
        new_win = new_win[:, -min(window, new_win.shape[1]):]
    x = _matmul([o], [_W(dense["nsa_out"], (j,))], [(0, 0, 0)], 1, _ep_resid, x.shape[1], extras=[(x, "mn")])
    return x, kv.reshape(batch, t, 4, kvh, hd), new_win


def _trunk(x, p, prm, dense, ab, nsa, states, past, dims, batch):
    depth = prm["norm_g"].shape[0]
    kv_rows, win_new, conv_new, sconv_new, ssm_new = [], [], [], [], []
    for i in range(depth):
        j = i // 2
        g = prm["norm_g"][i]
        x = _ffn(x, g[0], dense, i, 0, dims)
        h = _rmsnorm(x, g[1])
        if i % 2 == 0:
            x, cb, scb, sh = _ab_mixer(x, h, dense, ab[j], prm, j, states[j], dims, batch)
            conv_new.append(cb)
            sconv_new.append(scb)
            ssm_new.append(sh)
        else:
            x, rows, wb = _nsa_mixer(x, h, dense, nsa[j], j, past, dims, batch)
            kv_rows.append(rows)
            win_new.append(wb)
        x = _ffn(x, g[2], dense, i, 1, dims)
        hn = _rmsnorm(x, g[3])
        pp = _matmul([p[i].astype(bf16)], [_W(dense["ple_proj"], (i,))], [(0, 0, 0)], 1, _ep_plain, x.shape[1])
        x = _matmul([hn], [_W(dense["ple_gate"], (i,))], [(0, 0, 0)], 1, _ep_ple, x.shape[1],
                    extras=[(x, "mn"), (pp, "mn")])
    y = _rmsnorm(x, prm["final_norm_g"], out_dtype=f32)
    return (y, jnp.stack(kv_rows), jnp.stack(win_new), jnp.stack(conv_new), jnp.stack(sconv_new),
            jnp.stack(ssm_new))


def kernel(x_prompt, x_sample, cache_nsa_kv, cache_nsa_win, state_conv, state_ssm_conv, state_ssm, page_table,
           p_prompt, p_sample, norm_g, final_norm_g, ffn_w_up, ffn_w_down, ple_w_gate, ple_w_proj, ab_w_in,
           ab_w_out, conv_dw_w, conv_dw_b, conv_ln_g, conv_ln_b, ssm_conv_w, ssm_conv_b, ssm_dt_bias, ssm_a_log,
           ssm_d, ssm_norm_g, nsa_w_in, nsa_w_out, nsa_cmp_w):
    bp, seq, d_model = x_prompt.shape
    bs, dec_seq, _ = x_sample.shape
    depth = norm_g.shape[0]
    n_ab, _, heads, p_dim, n_state = state_ssm.shape
    n_c = cache_nsa_kv.shape[0]
    kvh, hd = cache_nsa_kv.shape[-2:]
    d_inner = heads * p_dim
    xbc = state_ssm_conv.shape[-1]
    ffn = ffn_w_down.shape[2]
    dims = {
        "ffn": ffn, "ffn_pad": -(-ffn // 1024) * 1024,
        "conv_ch": state_conv.shape[-1], "d_inner": d_inner, "xbc": xbc,
        "groups": (xbc - d_inner) // (2 * n_state),
        "qd": nsa_w_out.shape[1], "kvh": kvh, "hd": hd, "n_heads": nsa_w_out.shape[1] // hd,
        "window": cache_nsa_win.shape[2],
    }
    prm = {
        "norm_g": norm_g, "final_norm_g": final_norm_g, "conv_dw_w": conv_dw_w, "conv_dw_b": conv_dw_b,
        "conv_ln_g": conv_ln_g, "conv_ln_b": conv_ln_b, "ssm_conv_w": ssm_conv_w, "ssm_conv_b": ssm_conv_b,
        "ssm_norm_g": ssm_norm_g,
    }
    dense = _prep_dense(dims, ffn_w_up, ffn_w_down, ple_w_gate, ple_w_proj, ab_w_in, ab_w_out, nsa_w_in, nsa_w_out)
    ab = [_prep_ab(j, dims, ab_w_in, ssm_dt_bias, ssm_a_log, ssm_d) for j in range(n_ab)]
    nsa = [_prep_nsa(j, dims, nsa_w_in, nsa_cmp_w) for j in range(n_c)]

    zero_states = [
        (jnp.zeros((bp,) + state_conv.shape[2:], f32), jnp.zeros((bp,) + state_ssm_conv.shape[2:], f32),
         jnp.zeros((bp, heads, p_dim, n_state), f32))
        for _ in range(n_ab)
    ]
    y_p, kv_p, win_p, conv_p, sconv_p, ssm_p = _trunk(
        x_prompt.reshape(bp * seq, d_model), p_prompt.reshape(depth, bp * seq, -1), prm, dense, ab, nsa,
        zero_states, None, dims, bp)

    assert dec_seq == 1
    states = [(state_conv[j], state_ssm_conv[j], state_ssm[j]) for j in range(n_ab)]
    y_s, kv_s, win_s, conv_s, sconv_s, ssm_s = _trunk(
        x_sample.reshape(bs * dec_seq, d_model), p_sample.reshape(depth, bs * dec_seq, -1), prm, dense, ab, nsa,
        states, (cache_nsa_kv, cache_nsa_win, page_table), dims, bs)

    return (y_p.reshape(bp, seq, d_model), y_s.reshape(bs, dec_seq, d_model), kv_p, kv_s, win_p, win_s,
            conv_p, conv_s, sconv_p, sconv_s, ssm_p, ssm_s)
```

```python
import functools
from typing import NamedTuple

import jax
import jax.numpy as jnp
from jax import lax
from jax.experimental import pallas as pl
from jax.experimental.pallas import tpu as pltpu

f32 = jnp.float32
bf16 = jnp.bfloat16

V7X_LANES = 128
V7X_SUBLANES = 8
V7X_VMEM_BYTES = 64 * 1024 * 1024
VMEM_BUDGET = 46 * 1024 * 1024
VMEM_LIMIT = 56 * 1024 * 1024

NORM_EPS = 1e-6
CMP_STRIDE = 16
SEL_BLOCK = 64
SEL_TOPK = 16
SSM_CHUNK = 128
NEG = -1e30
FORCE_SCORE = 1e6


def _cparams(sem):
    return pltpu.CompilerParams(dimension_semantics=sem, vmem_limit_bytes=VMEM_LIMIT)


def _sigmoid(x):
    return jax.nn.sigmoid(x)


def _softplus(x):
    return jnp.maximum(x, 0.0) + jnp.log(1.0 + jnp.exp(-jnp.abs(x)))


def _mm_kernel(*refs, n_x, n_w, n_extra, n_acc, terms, epilogue, nk):
    xs = refs[:n_x]
    ws = refs[n_x:n_x + n_w]
    extras = refs[n_x + n_w:n_x + n_w + n_extra]
    out = refs[n_x + n_w + n_extra]
    accs = refs[n_x + n_w + n_extra + 1:]

    def partials():
        res = [None] * n_acc
        for xi, wi, ai in terms:
            d = jnp.dot(xs[xi][...], ws[wi][...], preferred_element_type=f32)
            res[ai] = d if res[ai] is None else res[ai] + d
        return res

    if nk == 1:
        out[...] = epilogue(partials(), [e[...] for e in extras]).astype(out.dtype)
        return

    k = pl.program_id(2)
    parts = partials()

    @pl.when(k == 0)
    def _():
        for a, p in zip(accs, parts):
            a[...] = p

    @pl.when(jnp.logical_and(k > 0, k < nk - 1))
    def _():
        for a, p in zip(accs, parts):
            a[...] += p

    @pl.when(k == nk - 1)
    def _():
        full = [a[...] + p for a, p in zip(accs, parts)]
        out[...] = epilogue(full, [e[...] for e in extras]).astype(out.dtype)


class _X(NamedTuple):
    arr: jax.Array
    width: int | None = None
    col_off: int = 0


class _W(NamedTuple):
    arr: jax.Array
    lead: tuple = ()
    rows: int | None = None
    row_off: int = 0
    col_off: int = 0


def _matmul(xs, ws, terms, n_acc, epilogue, n_cols, *, extras=(), out_dtype=f32, nk=1, bn_cap=1024, bm_cap=1024):
    xs = [x if isinstance(x, _X) else _X(x) for x in xs]
    m = xs[0].arr.shape[0]
    bm = next((c for c in (bm_cap, 1024) if m % c == 0), m)
    bm = min(bm, m)
    assert m % bm == 0
    xk = [(x.width or x.arr.shape[1]) // nk for x in xs]
    wk = [(w.rows or w.arr.shape[-2]) // nk for w in ws]
    xbytes = sum(bm * k * x.arr.dtype.itemsize for x, k in zip(xs, xk))

    def est(bn):
        wbytes = sum(k * bn * w.arr.dtype.itemsize for w, k in zip(ws, wk))
        ebytes = sum((bm if kind == "mn" else 1) * bn * e.dtype.itemsize for e, kind in extras)
        obytes = bm * bn * jnp.dtype(out_dtype).itemsize
        abytes = n_acc * bm * bn * 4 * (2 if nk > 1 else 1)
        return 2 * (xbytes + wbytes + ebytes + obytes) + abytes

    cands = [c for c in (2048, 1024, 512, 256, 128)
             if c <= bn_cap and n_cols % c == 0 and all(w.col_off % c == 0 for w in ws)]
    bn = next((c for c in cands if est(c) <= VMEM_BUDGET), cands[-1])
    grid = (m // bm, n_cols // bn, nk)

    in_specs = []
    for x, bk in zip(xs, xk):
        assert x.col_off % bk == 0
        in_specs.append(pl.BlockSpec((bm, bk), lambda i, j, k, o=x.col_off // bk: (i, o + k)))
    for w, bk in zip(ws, wk):
        assert w.row_off % bk == 0
        in_specs.append(pl.BlockSpec(
            (None,) * len(w.lead) + (bk, bn),
            lambda i, j, k, lead=tuple(w.lead), ro=w.row_off // bk, co=w.col_off // bn: lead + (ro + k, co + j)))
    for e, kind in extras:
        if kind == "mn":
            in_specs.append(pl.BlockSpec((bm, bn), lambda i, j, k: (i, j)))
        else:
            in_specs.append(pl.BlockSpec((1, bn), lambda i, j, k: (0, j)))
    scratch = [pltpu.VMEM((bm, bn), f32) for _ in range(n_acc)] if nk > 1 else []
    kern = functools.partial(_mm_kernel, n_x=len(xs), n_w=len(ws), n_extra=len(extras), n_acc=n_acc,
                             terms=tuple(terms), epilogue=epilogue, nk=nk)
    return pl.pallas_call(
        kern,
        out_shape=jax.ShapeDtypeStruct((m, n_cols), out_dtype),
        grid=grid,
        in_specs=in_specs,
        out_specs=pl.BlockSpec((bm, bn), lambda i, j, k: (i, j)),
        scratch_shapes=scratch,
        compiler_params=_cparams(("parallel", "parallel", "arbitrary")),
        name="mm",
    )(*[x.arr for x in xs], *[w.arr for w in ws], *[e for e, _ in extras])


def _ep_plain(accs, extras):
    return accs[0]


def _ep_swiglu(accs, extras):
    a, b = accs
    return a * _sigmoid(a) * b


def _ep_glu(accs, extras):
    return accs[0] * _sigmoid(accs[1])


def _ep_sigmoid(accs, extras):
    return _sigmoid(accs[0])


def _ep_softplus_bias(accs, extras):
    return _softplus(accs[0] + extras[0])


def _ep_half_resid(accs, extras):
    return extras[0] + 0.5 * accs[0]


def _ep_resid(accs, extras):
    return extras[0] + accs[0]


def _ep_ple(accs, extras):
    return extras[0] + _sigmoid(accs[0]) * extras[1]


def _rmsnorm_kernel(x_ref, g_ref, o_ref):
    x = x_ref[...]
    y = x * lax.rsqrt(jnp.mean(x * x, axis=-1, keepdims=True) + NORM_EPS)
    o_ref[...] = (y * g_ref[...]).astype(o_ref.dtype)


def _ln_silu_kernel(x_ref, g_ref, b_ref, o_ref):
    x = x_ref[...]
    xc = x - jnp.mean(x, axis=-1, keepdims=True)
    var = jnp.mean(xc * xc, axis=-1, keepdims=True)
    y = xc * lax.rsqrt(var + NORM_EPS) * g_ref[...] + b_ref[...]
    o_ref[...] = (y * _sigmoid(y)).astype(o_ref.dtype)


def _rowwise(kern, x, vecs, out_dtype, name):
    m, d = x.shape
    bm = m if m <= 256 else 256
    assert m % bm == 0
    return pl.pallas_call(
        kern,
        out_shape=jax.ShapeDtypeStruct((m, d), out_dtype),
        grid=(m // bm,),
        in_specs=[pl.BlockSpec((bm, d), lambda i: (i, 0))] + [pl.BlockSpec((1, d), lambda i: (0, 0)) for _ in vecs],
        out_specs=pl.BlockSpec((bm, d), lambda i: (i, 0)),
        compiler_params=_cparams(("parallel",)),
        name=name,
    )(x, *[v.reshape(1, d).astype(f32) for v in vecs])


def _rmsnorm(x, g, out_dtype=bf16):
    return _rowwise(_rmsnorm_kernel, x, [g], out_dtype, "rmsnorm")


def _ln_silu(x, g, b):
    return _rowwise(_ln_silu_kernel, x, [g, b], bf16, "ln_silu")


def _conv_kernel(u_ref, st_ref, w_ref, b_ref, o_ref, win_ref, *shifted, taps, halo, tt, silu):
    j = pl.program_id(2)
    sub = V7X_SUBLANES

    @pl.when(j == 0)
    def _():
        win_ref[0:halo, :] = st_ref[0]

    win_ref[halo:halo + tt, :] = u_ref[...]
    off = halo - (taps - 1)
    bias = b_ref[...]
    if shifted:
        sh_ref = shifted[0]
        span = halo + tt - sub
        for s in range(1, sub):
            sh_ref[s - 1] = win_ref[s:s + span, :]

        def tap(r0, k):
            a, s = divmod(off + k, sub)
            lo = r0 + sub * a
            return win_ref[lo:lo + sub, :] if s == 0 else sh_ref[s - 1, lo:lo + sub, :]
    else:
        def tap(r0, k):
            return win_ref[r0 + off + k:r0 + off + k + sub, :]

    for r in range(tt // V7X_SUBLANES):
        r0 = r * V7X_SUBLANES
        acc = jnp.broadcast_to(bias, (V7X_SUBLANES, bias.shape[1]))
        for k in range(taps):
            acc = acc + w_ref[k:k + 1, :] * tap(r0, k)
        if silu:
            acc = acc * _sigmoid(acc)
        o_ref[r0:r0 + V7X_SUBLANES, :] = acc.astype(o_ref.dtype)
    win_ref[0:halo, :] = win_ref[tt:tt + halo, :]


def _causal_conv(u, state, w, b, *, batch, silu, out_dtype=f32):
    m, c = u.shape
    t = m // batch
    taps = w.shape[0]
    halo = -(-(taps - 1) // V7X_SUBLANES) * V7X_SUBLANES
    use_shifted = taps > V7X_SUBLANES
    tt = min(t, 256 if use_shifted else 1024)
    assert t % tt == 0 and tt % V7X_SUBLANES == 0
    cc = 512 if c % 512 == 0 else V7X_LANES
    st = jnp.pad(state.astype(f32), ((0, 0), (halo - (taps - 1), 0), (0, 0)))
    nt = t // tt
    kern = functools.partial(_conv_kernel, taps=taps, halo=halo, tt=tt, silu=silu)
    scratch = [pltpu.VMEM((halo + tt, cc), f32)]
    if use_shifted:
        scratch.append(pltpu.VMEM((V7X_SUBLANES - 1, halo + tt - V7X_SUBLANES, cc), f32))
    return pl.pallas_call(
        kern,
        out_shape=jax.ShapeDtypeStruct((m, c), out_dtype),
        grid=(batch, c // cc, nt),
        in_specs=[
            pl.BlockSpec((tt, cc), lambda bi, ci, j: (bi * nt + j, ci)),
            pl.BlockSpec((1, halo, cc), lambda bi, ci, j: (bi, 0, ci)),
            pl.BlockSpec((taps, cc), lambda bi, ci, j: (0, ci)),
            pl.BlockSpec((1, cc), lambda bi, ci, j: (0, ci)),
        ],
        out_specs=pl.BlockSpec((tt, cc), lambda bi, ci, j: (bi * nt + j, ci)),
        scratch_shapes=scratch,
        compiler_params=_cparams(("parallel", "parallel", "arbitrary")),
        name="causal_conv",
    )(u, st, w.astype(f32), b.reshape(1, c).astype(f32))


def _split3(x):
    hi = x.astype(bf16)
    r1 = x - hi.astype(f32)
    mid = r1.astype(bf16)
    lo = (r1 - mid.astype(f32)).astype(bf16)
    return hi, mid, lo


def _dot_nt(a, b):
    return lax.dot_general(a, b, (((1,), (1,)), ((), ())), preferred_element_type=f32)


def _dot_tn(a, b):
    return lax.dot_general(a, b, (((0,), (0,)), ((), ())), preferred_element_type=f32)


def _ssd_kernel(xs_ref, b_ref, c_ref, dt_ref, z_ref, a_ref, d_ref, ng_ref, h0_ref, y_ref, hout_ref, h_scr,
                *, q, e_heads, p_dim, nc):
    c = pl.program_id(2)

    n = b_ref.shape[1]
    gw = e_heads * p_dim
    hb = V7X_LANES // p_dim

    @pl.when(c == 0)
    def _():
        h_scr[...] = h0_ref[0].reshape(gw, n)

    xs = xs_ref[...]
    bb = b_ref[...].astype(bf16)
    cb16 = c_ref[...].astype(bf16)
    dt = dt_ref[...]
    la = dt * a_ref[0]
    ri = lax.broadcasted_iota(jnp.int32, (q, q), 0)
    ci = lax.broadcasted_iota(jnp.int32, (q, q), 1)
    causal = ri >= ci
    tri = jnp.where(causal, 1.0, 0.0).astype(bf16)

    def dot3(v, mat):
        hi, mid, lo = _split3(v)
        return (jnp.dot(hi, mat, preferred_element_type=f32) + jnp.dot(mid, mat, preferred_element_type=f32)
                + jnp.dot(lo, mat, preferred_element_type=f32))

    def spread(width):
        src = lax.broadcasted_iota(jnp.int32, (V7X_LANES, e_heads * width), 0)
        dst = lax.shift_right_logical(lax.broadcasted_iota(jnp.int32, (V7X_LANES, e_heads * width), 1), _log2(width))
        return jnp.where(src == dst, 1.0, 0.0).astype(bf16)

    hi, mid, lo = _split3(la)
    cs = (jnp.dot(tri, hi, preferred_element_type=f32) + jnp.dot(tri, mid, preferred_element_type=f32)
          + jnp.dot(tri, lo, preferred_element_type=f32))
    cs_t = cs.T
    to_p = spread(p_dim)
    to_q = spread(q)
    dt_p = dot3(dt, to_p)
    cs_p = dot3(cs, to_p)
    cs_q = dot3(cs, to_q)
    d_p = dot3(jnp.broadcast_to(d_ref[0], (V7X_SUBLANES, V7X_LANES)), to_p)[0:1]
    xd = xs * dt_p
    xd16 = xd.astype(bf16)
    xw16 = (xd * jnp.exp(cs_p[q - 1:q, :] - cs_p)).astype(bf16)
    grow = jnp.exp(cs_p)
    cb = _dot_nt(cb16, bb)
    lane_head = lax.shift_right_logical(lax.broadcasted_iota(jnp.int32, (q, V7X_LANES), 1), _log2(p_dim))
    ys = []
    for pr in range(e_heads // hb):
        blk = slice(pr * V7X_LANES, (pr + 1) * V7X_LANES)
        x2 = xd16[:, blk]
        yd = None
        keep = []
        for k in range(hb):
            e = pr * hb + k
            col = cs_q[:, e * q:(e + 1) * q]
            decay = jnp.exp(jnp.where(causal, col - cs_t[e:e + 1, :], NEG))
            yk = jnp.dot((cb * decay).astype(bf16), x2, preferred_element_type=f32)
            yd = yk if yd is None else jnp.where(lane_head == k, yk, yd)
            keep.append(jnp.broadcast_to(jnp.exp(col[q - 1:q, :]), (p_dim, n)))
        h2 = h_scr[blk, :]
        s2 = _dot_tn(xw16[:, blk], bb)
        ys.append(yd + _dot_nt(cb16, h2.astype(bf16)) * grow[:, blk])
        h_scr[blk, :] = h2 * jnp.concatenate(keep, axis=0) + s2
    y = jnp.concatenate(ys, axis=1) + d_p * xs
    z = z_ref[...]
    yg = y * (z * _sigmoid(z))
    yg = yg * lax.rsqrt(jnp.mean(yg * yg, axis=-1, keepdims=True) + NORM_EPS)
    y_ref[...] = (yg * ng_ref[...]).astype(y_ref.dtype)

    @pl.when(c == nc - 1)
    def _():
        hout_ref[0] = h_scr[...].reshape(e_heads, p_dim, n)


def _ssd(xbc, dtg, z, a_g, d_g, norm_g, h0, *, batch, groups):
    m = xbc.shape[0]
    t = m // batch
    _, heads, p_dim, n = h0.shape
    d_inner = heads * p_dim
    e_heads = heads // groups
    gw = e_heads * p_dim
    q = SSM_CHUNK
    assert t % q == 0 and gw % V7X_LANES == 0 and n == q and V7X_LANES % p_dim == 0
    assert e_heads % (V7X_LANES // p_dim) == 0 and e_heads <= V7X_LANES
    nc = t // q
    b_off = d_inner // n
    c_off = b_off + groups
    kern = functools.partial(_ssd_kernel, q=q, e_heads=e_heads, p_dim=p_dim, nc=nc)
    return pl.pallas_call(
        kern,
        out_shape=(jax.ShapeDtypeStruct((m, d_inner), bf16), jax.ShapeDtypeStruct((batch, heads, p_dim, n), f32)),
        grid=(batch, groups, nc),
        in_specs=[
            pl.BlockSpec((q, gw), lambda bi, g, c: (bi * nc + c, g)),
            pl.BlockSpec((q, n), lambda bi, g, c: (bi * nc + c, b_off + g)),
            pl.BlockSpec((q, n), lambda bi, g, c: (bi * nc + c, c_off + g)),
            pl.BlockSpec((q, V7X_LANES), lambda bi, g, c: (bi * nc + c, g)),
            pl.BlockSpec((q, gw), lambda bi, g, c: (bi * nc + c, g)),
            pl.BlockSpec((1, 1, V7X_LANES), lambda bi, g, c: (g, 0, 0)),
            pl.BlockSpec((1, 1, V7X_LANES), lambda bi, g, c: (g, 0, 0)),
            pl.BlockSpec((1, gw), lambda bi, g, c: (0, g)),
            pl.BlockSpec((1, e_heads, p_dim, n), lambda bi, g, c: (bi, g, 0, 0)),
        ],
        out_specs=(
            pl.BlockSpec((q, gw), lambda bi, g, c: (bi * nc + c, g)),
            pl.BlockSpec((1, e_heads, p_dim, n), lambda bi, g, c: (bi, g, 0, 0)),
        ),
        scratch_shapes=[pltpu.VMEM((gw, n), f32)],
        compiler_params=_cparams(("parallel", "parallel", "arbitrary")),
        name="ssd",
    )(xbc, xbc, xbc, dtg, z, a_g, d_g, norm_g.reshape(1, d_inner).astype(f32), h0)


def _log2(v):
    assert v > 0 and v & (v - 1) == 0, v
    return v.bit_length() - 1


def _masked_softmax(s, mask):
    s = jnp.where(mask, s, NEG)
    m = jnp.max(s, axis=-1, keepdims=True)
    e = jnp.where(mask, jnp.exp(s - m), 0.0)
    return e / jnp.maximum(jnp.sum(e, axis=-1, keepdims=True), 1e-30)


def _compress_partial(rows, wh, wt, stride):
    t, c = rows.shape
    r = rows.reshape(t // stride, stride, c)
    return jnp.sum(r * wh[None], axis=1), jnp.sum(r * wt[None], axis=1)


def _compress_combine(head, tail):
    n, c = head.shape
    nxt = pltpu.roll(tail, n - 1, 0)
    keep = lax.broadcasted_iota(jnp.int32, (n, c), 0) < n - 1
    return head + jnp.where(keep, nxt, 0.0)


def _compress_kernel(rows_ref, wh_ref, wt_ref, o_ref, *, stride):
    head, tail = _compress_partial(rows_ref[...], wh_ref[...], wt_ref[...], stride)
    o_ref[0] = _compress_combine(head, tail)


def _compress(kv, wh, wt, *, batch):
    m = kv.shape[0]
    t = m // batch
    stride, c2 = wh.shape
    cc = 256
    assert c2 % cc == 0 and t % stride == 0
    return pl.pallas_call(
        functools.partial(_compress_kernel, stride=stride),
        out_shape=jax.ShapeDtypeStruct((batch, t // stride, c2), f32),
        grid=(batch, c2 // cc),
        in_specs=[
            pl.BlockSpec((t, cc), lambda bi, ci: (bi, ci)),
            pl.BlockSpec((stride, cc), lambda bi, ci: (0, ci)),
            pl.BlockSpec((stride, cc), lambda bi, ci: (0, ci)),
        ],
        out_specs=pl.BlockSpec((1, t // stride, cc), lambda bi, ci: (bi, 0, ci)),
        compiler_params=_cparams(("parallel", "parallel")),
        name="nsa_compress",
    )(kv, wh, wt)


_LOG2E = 1.4426950408889634


def _nsa_prompt_kernel(q_ref, ks_ref, vs_ref, kw_ref, vw_ref, kc_ref, vc_ref, gate_ref, o_ref, tm_scr,
                       *, tq, e_heads, hd, n_heads, t, window, cmp_len, tk_sel):
    g = pl.program_id(1)
    i = pl.program_id(2)
    q0 = i * tq
    rows = e_heads * tq
    scale = hd ** -0.5
    q = q_ref[...]
    qs = jnp.concatenate([q[:, e * hd:(e + 1) * hd] for e in range(e_heads)], axis=0)
    r_idx = lax.broadcasted_iota(jnp.int32, (rows, 1), 0)
    e_idx = lax.shift_right_logical(r_idx, _log2(tq))
    t_idx = r_idx - e_idx * tq
    slope = jnp.exp2(-8.0 * (g * e_heads + e_idx + 1).astype(f32) / n_heads)
    qpos = q0 + t_idx

    nc = kc_ref.shape[1]
    kc = kc_ref[0].astype(bf16)
    vc = vc_ref[0].astype(bf16)
    s = _dot_nt(qs, kc) * scale
    c_end = lax.broadcasted_iota(jnp.int32, (1, nc), 1) * CMP_STRIDE + (cmp_len - 1)
    dist = qpos - c_end
    s = s - slope * dist.astype(f32)
    p_c = _masked_softmax(s, dist >= 0)
    o_c = jnp.dot(p_c.astype(bf16), vc, preferred_element_type=f32)
    imp = jnp.sum(p_c.reshape(e_heads, tq, nc), axis=0)

    n_sel = -(-t // SEL_BLOCK)
    k_top = min(SEL_TOPK, n_sel)
    nsp = -(-n_sel // V7X_SUBLANES) * V7X_SUBLANES
    shift = _log2(SEL_BLOCK // CMP_STRIDE)
    gj = lax.broadcasted_iota(jnp.int32, (nsp, nc), 0)
    gi = lax.shift_right_logical(lax.broadcasted_iota(jnp.int32, (nsp, nc), 1), shift)
    gm_t = jnp.where(gi == gj, 1.0, 0.0).astype(bf16)
    hi, mid, lo = _split3(imp)
    bimp_t = _dot_nt(gm_t, hi) + _dot_nt(gm_t, mid) + _dot_nt(gm_t, lo)
    jl = lax.broadcasted_iota(jnp.int32, (nsp, tq), 0)
    tpos = q0 + lax.broadcasted_iota(jnp.int32, (nsp, tq), 1)
    cur = lax.shift_right_logical(tpos, _log2(SEL_BLOCK))
    valid = jl * SEL_BLOCK <= tpos
    forced = (jl == 0) | (jl == cur) | (jl == cur - 1)
    score = jnp.where(valid, jnp.where(forced, FORCE_SCORE, bimp_t), -FORCE_SCORE)
    rank = jnp.zeros((nsp, tq), f32)
    for j2 in range(n_sel):
        other = score[j2:j2 + 1, :]
        beats = (other > score) | ((other == score) & (jl > j2))
        rank = rank + jnp.where(beats, 1.0, 0.0)
    sel_t = jnp.where(rank < k_top, 1.0, 0.0).astype(bf16)

    wk = tm_scr.shape[1]

    @pl.when(i == 0)
    def _():
        rel = (lax.broadcasted_iota(jnp.int32, (rows, wk), 1) - t_idx).astype(f32)
        tm_scr[...] = slope * (rel * _LOG2E)

    c_scale = scale * _LOG2E

    def sel_body(kt, carry):
        m, l, acc = carry
        k0 = pl.multiple_of(kt * tk_sel, tk_sel)
        kb = ks_ref[pl.ds(k0, tk_sel), :]
        vb = vs_ref[pl.ds(k0, tk_sel), :]
        s2 = _dot_nt(qs, kb) * c_scale + tm_scr[:, 0:tk_sel]
        jj = lax.broadcasted_iota(jnp.int32, (nsp, tk_sel), 0)
        kk = lax.shift_right_logical(k0 + lax.broadcasted_iota(jnp.int32, (nsp, tk_sel), 1), _log2(SEL_BLOCK))
        expand = jnp.where(jj == kk, 1.0, 0.0).astype(bf16)
        ms = _dot_tn(sel_t, expand)
        tt_ = q0 + lax.broadcasted_iota(jnp.int32, (tq, tk_sel), 0)
        cc_ = k0 + lax.broadcasted_iota(jnp.int32, (tq, tk_sel), 1)
        madd = jnp.where((ms > 0.5) & (cc_ <= tt_), 0.0, NEG)
        s3 = (s2.reshape(e_heads, tq, tk_sel) + madd[None]).reshape(rows, tk_sel)
        rowoff = slope * ((k0 - q0).astype(f32) * _LOG2E)
        m_new = jnp.maximum(m, jnp.max(s3, axis=-1, keepdims=True) + rowoff)
        p = jnp.exp2(s3 - (m_new - rowoff))
        alpha = jnp.exp2(m - m_new)
        l = alpha * l + jnp.sum(p, axis=-1, keepdims=True)
        acc = alpha * acc + jnp.dot(p.astype(bf16), vb, preferred_element_type=f32)
        return m_new, l, acc

    init = (jnp.full((rows, 1), NEG, f32), jnp.zeros((rows, 1), f32), jnp.zeros((rows, hd), f32))
    hi_sel = (q0 + tq + tk_sel - 1) // tk_sel
    _, l_s, acc_s = lax.fori_loop(0, hi_sel, sel_body, init)
    o_s = acc_s / jnp.maximum(l_s, 1e-30)

    start = pl.multiple_of(jnp.clip(q0 - window, 0, t - wk), tq)
    kb = kw_ref[pl.ds(start, wk), :]
    vb = vw_ref[pl.ds(start, wk), :]
    s2 = _dot_nt(qs, kb) * c_scale + tm_scr[...]
    dw = (q0 + lax.broadcasted_iota(jnp.int32, (tq, wk), 0)) - (start + lax.broadcasted_iota(jnp.int32, (tq, wk), 1))
    madd = jnp.where((dw >= 0) & (dw < window), 0.0, NEG)
    s3 = (s2.reshape(e_heads, tq, wk) + madd[None]).reshape(rows, wk)
    p = jnp.exp2(s3 - jnp.max(s3, axis=-1, keepdims=True))
    o_w = (jnp.dot(p.astype(bf16), vb, preferred_element_type=f32)
           / jnp.maximum(jnp.sum(p, axis=-1, keepdims=True), 1e-30))

    gates = gate_ref[...]
    outs = []
    for e in range(e_heads):
        sl = slice(e * tq, (e + 1) * tq)
        outs.append(gates[:, e:e + 1] * o_c[sl] + gates[:, e_heads + e:e_heads + e + 1] * o_s[sl]
                    + gates[:, 2 * e_heads + e:2 * e_heads + e + 1] * o_w[sl])
    o_ref[...] = jnp.concatenate(outs, axis=1).astype(o_ref.dtype)


def _nsa_prompt(q, kvb, winb, cmp, gate, *, batch, kvh, hd, window, cmp_len):
    m, qd = q.shape
    t = m // batch
    n_heads = qd // hd
    e_heads = n_heads // kvh
    tq = 128
    tk_sel = 256
    wk = window + tq
    assert t % tq == 0 and t % tk_sel == 0 and window % tq == 0 and t >= wk and wk >= tk_sel
    nq = t // tq
    nc = cmp.shape[1]
    gw = e_heads * hd
    kern = functools.partial(_nsa_prompt_kernel, tq=tq, e_heads=e_heads, hd=hd, n_heads=n_heads, t=t, window=window,
                             cmp_len=cmp_len, tk_sel=tk_sel)
    return pl.pallas_call(
        kern,
        out_shape=jax.ShapeDtypeStruct((m, qd), bf16),
        grid=(batch, kvh, nq),
        in_specs=[
            pl.BlockSpec((tq, gw), lambda bi, g, i: (bi * nq + i, g)),
            pl.BlockSpec((t, hd), lambda bi, g, i: (bi, 2 * kvh + g)),
            pl.BlockSpec((t, hd), lambda bi, g, i: (bi, 3 * kvh + g)),
            pl.BlockSpec((t, hd), lambda bi, g, i: (bi, g)),
            pl.BlockSpec((t, hd), lambda bi, g, i: (bi, kvh + g)),
            pl.BlockSpec((1, nc, hd), lambda bi, g, i: (bi, 0, g)),
            pl.BlockSpec((1, nc, hd), lambda bi, g, i: (bi, 0, kvh + g)),
            pl.BlockSpec((tq, V7X_LANES), lambda bi, g, i: (bi * nq + i, g)),
        ],
        out_specs=pl.BlockSpec((tq, gw), lambda bi, g, i: (bi * nq + i, g)),
        scratch_shapes=[pltpu.VMEM((e_heads * tq, wk), f32)],
        compiler_params=_cparams(("parallel", "parallel", "arbitrary")),
        name="nsa_prompt",
    )(q, kvb, kvb, winb, winb, cmp, cmp, gate)


def _nsa_decode_kernel(pt_ref, *refs, pps, kvh, e_heads, hd, n_heads, past, window, cmp_len, n_steps):
    page_refs = refs[:pps]
    q_ref, kvn_ref, wn_ref, gate_ref, cwin_ref, wh_ref, wt_ref, o_ref = refs[pps:pps + 8]
    head_scr, tail_scr, sel_scr, m_scr, l_scr, acc_scr, oc_scr = refs[pps + 8:]
    ps = pl.program_id(1)
    st = pl.program_id(2)
    scale = hd ** -0.5
    page_rows = page_refs[0].shape[2]
    c_rows = 2 * kvh
    c_shift = _log2(c_rows)
    sub = page_rows // CMP_STRIDE
    n_sub = past // CMP_STRIDE
    n_sel = past // SEL_BLOCK + 1
    k_top = min(SEL_TOPK, n_sel)
    n_sel_pad = sel_scr.shape[1]
    cur = past // SEL_BLOCK

    h_idx = lax.broadcasted_iota(jnp.int32, (n_heads, 1), 0)
    g_idx = lax.shift_right_logical(h_idx, _log2(e_heads))
    slope = jnp.exp2(-8.0 * (h_idx + 1).astype(f32) / n_heads)

    def rnd(x):
        return x.astype(bf16).astype(f32)

    def per_head(rows):
        return jnp.concatenate([jnp.broadcast_to(rows[g:g + 1], (e_heads, hd)) for g in range(kvh)], axis=0)

    def own_key_columns(n_cols):
        col = lax.broadcasted_iota(jnp.int32, (1, n_cols), 1)
        return lax.shift_right_logical(col, c_shift), (col & (c_rows - 1)) == g_idx

    @pl.when(ps == 0)
    def _():
        wh = wh_ref[...]
        wt = wt_ref[...]
        for r in range(pps):
            row0 = (st * pps + r) * sub
            blk = page_refs[r][0, 0].reshape(sub, CMP_STRIDE, c_rows, hd)
            head_scr[pl.ds(row0, sub)] = jnp.sum(blk * wh[None], axis=1)
            tail_scr[pl.ds(row0, sub)] = jnp.sum(blk * wt[None], axis=1)

    @pl.when(jnp.logical_and(ps == 1, st == 0))
    def _():
        tail = tail_scr[...]
        cmp = head_scr[...] + jnp.concatenate([tail[1:], jnp.zeros((1, c_rows, hd), f32)], axis=0)
        cmp2 = cmp.reshape(n_sub * c_rows, hd).astype(bf16)
        qb = q_ref[0].astype(bf16)
        i_idx, own = own_key_columns(n_sub * c_rows)
        dist = past - (i_idx * CMP_STRIDE + (cmp_len - 1))
        s = _dot_nt(qb, cmp2) * scale - slope * dist.astype(f32)
        p = _masked_softmax(s, own & (dist >= 0))
        oc_scr[...] = jnp.dot(pltpu.roll(p, kvh, 1).astype(bf16), cmp2, preferred_element_type=f32)
        imp = jnp.sum(p.reshape(kvh, e_heads, n_sub * c_rows), axis=1)
        imp = jnp.concatenate([imp, jnp.zeros((V7X_SUBLANES - kvh, n_sub * c_rows), f32)], axis=0)
        shift = _log2(SEL_BLOCK // CMP_STRIDE) + c_shift
        ii = lax.broadcasted_iota(jnp.int32, (n_sub * c_rows, n_sel_pad), 0)
        jj = lax.broadcasted_iota(jnp.int32, (n_sub * c_rows, n_sel_pad), 1)
        gm = jnp.where(lax.shift_right_logical(ii, shift) == jj, 1.0, 0.0).astype(bf16)
        hi, mid, lo = _split3(imp)
        bimp = (jnp.dot(hi, gm, preferred_element_type=f32) + jnp.dot(mid, gm, preferred_element_type=f32)
                + jnp.dot(lo, gm, preferred_element_type=f32))
        jl = lax.broadcasted_iota(jnp.int32, (V7X_SUBLANES, n_sel_pad), 1)
        valid = jl * SEL_BLOCK <= past
        forced = (jl == 0) | (jl == cur) | (jl == cur - 1)
        score = jnp.where(valid, jnp.where(forced, FORCE_SCORE, bimp), -FORCE_SCORE)
        score = jnp.where(jl < n_sel, score, -2.0 * FORCE_SCORE)
        j2 = lax.broadcasted_iota(jnp.int32, (n_sel_pad, n_sel_pad), 0)
        j1 = lax.broadcasted_iota(jnp.int32, (n_sel_pad, n_sel_pad), 1)
        for g in range(kvh):
            sc_row = jnp.broadcast_to(score[g:g + 1], (n_sel_pad, n_sel_pad))
            sc_col = sc_row.T
            beats = ((sc_col > sc_row) | ((sc_col == sc_row) & (j2 < j1))) & (j2 < n_sel)
            rank = jnp.sum(jnp.where(beats, 1.0, 0.0), axis=0, keepdims=True)
            sel_scr[g * e_heads:(g + 1) * e_heads, :] = jnp.broadcast_to(jnp.where(rank < k_top, 1.0, 0.0),
                                                                          (e_heads, n_sel_pad))
        m_scr[...] = jnp.full(m_scr.shape, NEG, f32)
        l_scr[...] = jnp.zeros(l_scr.shape, f32)
        acc_scr[...] = jnp.zeros(acc_scr.shape, f32)

    @pl.when(ps == 1)
    def _():
        qb = q_ref[0].astype(bf16)
        sel = sel_scr[...]
        jl = lax.broadcasted_iota(jnp.int32, (1, n_sel_pad), 1)
        n_cols = page_rows * c_rows
        r_idx, own = own_key_columns(n_cols)
        r_blk = lax.shift_right_logical(r_idx, _log2(SEL_BLOCK))
        for r in range(pps):
            blk = page_refs[r][0, 0].reshape(n_cols, hd).astype(bf16)
            k0 = (st * pps + r) * page_rows
            d = past - (k0 + r_idx)
            blk0 = (st * pps + r) * (page_rows // SEL_BLOCK)
            picked = jnp.zeros((n_heads, n_cols), f32)
            for b in range(page_rows // SEL_BLOCK):
                flag = jnp.sum(jnp.where(jl == blk0 + b, sel, 0.0), axis=-1, keepdims=True)
                picked = jnp.where(r_blk == b, flag, picked)
            mask = own & (picked > 0.5) & (d >= 0)
            sc = jnp.where(mask, _dot_nt(qb, blk) * scale - slope * d.astype(f32), NEG)
            m = m_scr[...]
            m_new = jnp.maximum(m, jnp.max(sc, axis=-1, keepdims=True))
            alpha = jnp.exp(m - m_new)
            p = jnp.where(mask, jnp.exp(sc - m_new), 0.0)
            l_scr[...] = alpha * l_scr[...] + jnp.sum(p, axis=-1, keepdims=True)
            acc_scr[...] = alpha * acc_scr[...] + jnp.dot(pltpu.roll(p, kvh, 1).astype(bf16), blk,
                                                          preferred_element_type=f32)
            m_scr[...] = m_new

    @pl.when(jnp.logical_and(ps == 1, st == n_steps - 1))
    def _():
        qb = q_ref[0].astype(bf16)
        q32 = qb.astype(f32)
        kn = per_head(rnd(kvn_ref[0, 2 * kvh:3 * kvh, :]))
        vn = per_head(rnd(kvn_ref[0, 3 * kvh:4 * kvh, :]))
        sel_new = sel_scr[:, n_sel - 1:n_sel] > 0.5
        s_n = jnp.where(sel_new, jnp.sum(q32 * kn, axis=-1, keepdims=True) * scale, NEG)
        m = m_scr[...]
        m_new = jnp.maximum(m, s_n)
        alpha = jnp.exp(m - m_new)
        p_n = jnp.where(sel_new, jnp.exp(s_n - m_new), 0.0)
        l = alpha * l_scr[...] + p_n
        o_s = (alpha * acc_scr[...] + rnd(p_n) * vn) / jnp.maximum(l, 1e-30)
        pb = cwin_ref.shape[2]
        cw = cwin_ref[0, 0].reshape(pb * c_rows, hd).astype(bf16)
        r_idx, own = own_key_columns(pb * c_rows)
        wpos = past - pb + r_idx
        dw = past - wpos
        wmask = own & (dw >= 0) & (dw < window) & (wpos >= 0)
        sw = jnp.where(wmask, _dot_nt(qb, cw) * scale - slope * dw.astype(f32), NEG)
        kwn = per_head(rnd(wn_ref[0, 0:kvh, :]))
        vwn = per_head(rnd(wn_ref[0, kvh:2 * kvh, :]))
        sw_n = jnp.sum(q32 * kwn, axis=-1, keepdims=True) * scale
        mw = jnp.maximum(jnp.max(sw, axis=-1, keepdims=True), sw_n)
        ew = jnp.where(wmask, jnp.exp(sw - mw), 0.0)
        en = jnp.exp(sw_n - mw)
        den = jnp.maximum(jnp.sum(ew, axis=-1, keepdims=True) + en, 1e-30)
        o_w = (jnp.dot(pltpu.roll(ew, kvh, 1).astype(bf16), cw, preferred_element_type=f32) + rnd(en) * vwn) / den
        gates = gate_ref[0]
        o_ref[0] = gates[:, 0:1] * oc_scr[...] + gates[:, 1:2] * o_s + gates[:, 2:3] * o_w


def _nsa_decode(q, kvn, wn, gate, pool, cwin, page_table, wh, wt, *, layer, kvh, hd, window, cmp_len):
    batch, n_heads, _ = q.shape
    e_heads = n_heads // kvh
    n_pages = page_table.shape[1]
    page_rows = pool.shape[2]
    past = n_pages * page_rows
    c_rows = 2 * kvh
    assert page_rows % SEL_BLOCK == 0 and page_rows % CMP_STRIDE == 0 and c_rows == V7X_SUBLANES
    pps = 4 if n_pages % 4 == 0 else 1
    n_steps = n_pages // pps
    n_sub = past // CMP_STRIDE
    n_sel_pad = -(-(past // SEL_BLOCK + 1) // V7X_LANES) * V7X_LANES
    pb = cwin.shape[2]
    kern = functools.partial(_nsa_decode_kernel, pps=pps, kvh=kvh, e_heads=e_heads, hd=hd, n_heads=n_heads, past=past,
                             window=window, cmp_len=cmp_len, n_steps=n_steps)
    page_specs = [
        pl.BlockSpec((1, 1, page_rows, c_rows, hd),
                     lambda b, ps, st, pt, r=r: (layer, pt[b, st * pps + r], 0, ps, 0))
        for r in range(pps)
    ]
    grid_spec = pltpu.PrefetchScalarGridSpec(
        num_scalar_prefetch=1,
        grid=(batch, 2, n_steps),
        in_specs=page_specs + [
            pl.BlockSpec((1, n_heads, hd), lambda b, ps, st, pt: (b, 0, 0)),
            pl.BlockSpec((1, 4 * kvh, hd), lambda b, ps, st, pt: (b, 0, 0)),
            pl.BlockSpec((1, 2 * kvh, hd), lambda b, ps, st, pt: (b, 0, 0)),
            pl.BlockSpec((1, n_heads, V7X_LANES), lambda b, ps, st, pt: (b, 0, 0)),
            pl.BlockSpec((1, 1, pb, c_rows, hd), lambda b, ps, st, pt: (layer, b, 0, 0, 0)),
            pl.BlockSpec((CMP_STRIDE, c_rows, hd), lambda b, ps, st, pt: (0, 0, 0)),
            pl.BlockSpec((CMP_STRIDE, c_rows, hd), lambda b, ps, st, pt: (0, 0, 0)),
        ],
        out_specs=pl.BlockSpec((1, n_heads, hd), lambda b, ps, st, pt: (b, 0, 0)),
        scratch_shapes=[
            pltpu.VMEM((n_sub, c_rows, hd), f32),
            pltpu.VMEM((n_sub, c_rows, hd), f32),
            pltpu.VMEM((n_heads, n_sel_pad), f32),
            pltpu.VMEM((n_heads, 1), f32),
            pltpu.VMEM((n_heads, 1), f32),
            pltpu.VMEM((n_heads, hd), f32),
            pltpu.VMEM((n_heads, hd), f32),
        ],
    )
    return pl.pallas_call(
        kern,
        out_shape=jax.ShapeDtypeStruct((batch, n_heads, hd), f32),
        grid_spec=grid_spec,
        compiler_params=_cparams(("arbitrary", "arbitrary", "arbitrary")),
        name="nsa_decode",
    )(page_table, *([pool] * pps), q, kvn, wn, gate, cwin, wh, wt)


def _group_lanes(v, groups):
    e = v.shape[-1] // groups
    v = v.reshape(v.shape[:-1] + (groups, e))
    v = jnp.pad(v, [(0, 0)] * (v.ndim - 1) + [(0, V7X_LANES - e)])
    return v.reshape(v.shape[:-2] + (groups * V7X_LANES,))


def _prep_dense(dims, ffn_w_up, ffn_w_down, ple_w_gate, ple_w_proj, ab_w_in, ab_w_out, nsa_w_in, nsa_w_out):
    return {
        "up": ffn_w_up.astype(bf16),
        "down": ffn_w_down.astype(bf16),
        "ple_gate": ple_w_gate.astype(bf16),
        "ple_proj": ple_w_proj.astype(bf16),
        "ab_in": ab_w_in.astype(bf16),
        "ab_out": ab_w_out.astype(bf16),
        "nsa_in": nsa_w_in.astype(bf16),
        "nsa_out": nsa_w_out.astype(bf16),
    }


def _prep_ab(j, dims, ab_w_in, ssm_dt_bias, ssm_a_log, ssm_d):
    groups = dims["groups"]
    o3 = 2 * dims["conv_ch"] + dims["d_inner"] + dims["xbc"]
    return {
        "dt": _group_lanes(ab_w_in[j, :, o3:], groups).astype(bf16),
        "dt_bias": _group_lanes(ssm_dt_bias[j].astype(f32)[None], groups),
        "a": _group_lanes(-jnp.exp(ssm_a_log[j].astype(f32))[None], groups).reshape(groups, 1, V7X_LANES),
        "d": _group_lanes(ssm_d[j].astype(f32)[None], groups).reshape(groups, 1, V7X_LANES),
    }


def _prep_nsa(j, dims, nsa_w_in, nsa_cmp_w):
    qd, kvh, hd, n_heads = dims["qd"], dims["kvh"], dims["hd"], dims["n_heads"]
    e_heads = n_heads // kvh
    wg = nsa_w_in[j, :, qd + 6 * kvh * hd:]
    wg_grp = wg.reshape(-1, 3, kvh, e_heads).transpose(0, 2, 1, 3).reshape(-1, kvh * 3 * e_heads)
    cw = nsa_cmp_w[j].astype(f32)
    cmp_len = cw.shape[1]
    cw = cw.reshape(2, cmp_len, kvh * hd)
    half = cmp_len // 2
    return {
        "gate_grp": _group_lanes(wg_grp, kvh).astype(bf16),
        "gate_flat": jnp.pad(wg, ((0, 0), (0, V7X_LANES - wg.shape[1] % V7X_LANES))).astype(bf16),
        "wh": jnp.concatenate([cw[0, :half], cw[1, :half]], axis=1),
        "wt": jnp.concatenate([cw[0, half:], cw[1, half:]], axis=1),
        "cmp_len": cmp_len,
    }


_DOWN_BK_CAP = 5504


def _ffn(x, g, dense, i, s, dims):
    h = _rmsnorm(x, g)
    ffn = dims["ffn"]
    act = _matmul([h], [_W(dense["up"], (i, s)), _W(dense["up"], (i, s), col_off=ffn)], [(0, 0, 0), (0, 1, 1)], 2,
                  _ep_swiglu, ffn, out_dtype=bf16, bn_cap=512, bm_cap=2048)
    units = ffn // V7X_LANES
    nk = min(d for d in range(1, units + 1) if units % d == 0 and ffn // d <= _DOWN_BK_CAP)
    return _matmul([act], [_W(dense["down"], (i, s))], [(0, 0, 0)], 1, _ep_half_resid, x.shape[1],
                   extras=[(x, "mn")], nk=nk)


def _pad_rows(a, batch, rows):
    c = a.shape[1]
    return jnp.pad(a[:, None, :], ((0, 0), (0, rows - 1), (0, 0))).reshape(batch * rows, c)


def _ab_mixer(x, h, dense, w, prm, j, st, dims, batch):
    m = x.shape[0]
    t = m // batch
    ch, d_inner, groups = dims["conv_ch"], dims["d_inner"], dims["groups"]
    w_in = dense["ab_in"]
    u = _matmul([h], [_W(w_in, (j,)), _W(w_in, (j,), col_off=ch)], [(0, 0, 0), (0, 1, 1)], 2, _ep_glu, ch)
    z = _matmul([h], [_W(w_in, (j,), col_off=2 * ch)], [(0, 0, 0)], 1, _ep_plain, d_inner)
    xbc = _matmul([h], [_W(w_in, (j,), col_off=2 * ch + d_inner)], [(0, 0, 0)], 1, _ep_plain, dims["xbc"])
    dtg = _matmul([h], [_W(w["dt"])], [(0, 0, 0)], 1, _ep_softplus_bias, groups * V7X_LANES,
                  extras=[(w["dt_bias"], "n")])
    conv_state, sconv_state, ssm_state = st
    if t == 1:
        rc = V7X_SUBLANES
        c_pre = _causal_conv(_pad_rows(u, batch, rc), conv_state, prm["conv_dw_w"][j], prm["conv_dw_b"][j],
                             batch=batch, silu=False)[::rc]
        xbc_c = _causal_conv(_pad_rows(xbc, batch, rc), sconv_state, prm["ssm_conv_w"][j], prm["ssm_conv_b"][j],
                             batch=batch, silu=True)[::rc]
        rq = SSM_CHUNK
        y, h_new = _ssd(_pad_rows(xbc_c, batch, rq), _pad_rows(dtg, batch, rq), _pad_rows(z, batch, rq),
                        w["a"], w["d"], prm["ssm_norm_g"][j], ssm_state.astype(f32), batch=batch, groups=groups)
        y = y[::rq]
    else:
        c_pre = _causal_conv(u, conv_state, prm["conv_dw_w"][j], prm["conv_dw_b"][j], batch=batch, silu=False)
        xbc_c = _causal_conv(xbc, sconv_state, prm["ssm_conv_w"][j], prm["ssm_conv_b"][j], batch=batch, silu=True)
        y, h_new = _ssd(xbc_c, dtg, z, w["a"], w["d"], prm["ssm_norm_g"][j], ssm_state.astype(f32), batch=batch,
                        groups=groups)
    c = _ln_silu(c_pre, prm["conv_ln_g"][j], prm["conv_ln_b"][j])
    assert d_inner % ch == 0
    n_y = d_inner // ch
    xs = [c] + [_X(y, ch, r * ch) for r in range(n_y)]
    ws = [_W(dense["ab_out"], (j,), rows=ch, row_off=r * ch) for r in range(n_y + 1)]
    x = _matmul(xs, ws, [(r, r, 0) for r in range(n_y + 1)], 1, _ep_resid, x.shape[1], extras=[(x, "mn")])
    k1 = conv_state.shape[1]
    k2 = sconv_state.shape[1]
    u_ext = jnp.concatenate([conv_state.astype(f32), u.reshape(batch, t, -1)], axis=1)[:, -k1:]
    xbc_ext = jnp.concatenate([sconv_state.astype(f32), xbc.reshape(batch, t, -1)], axis=1)[:, -k2:]
    return x, u_ext, xbc_ext, h_new


def _nsa_mixer(x, h, dense, w, j, past, dims, batch):
    m = x.shape[0]
    t = m // batch
    qd, kvh, hd, n_heads, window = dims["qd"], dims["kvh"], dims["hd"], dims["n_heads"], dims["window"]
    w_in = dense["nsa_in"]
    kv = _matmul([h], [_W(w_in, (j,), col_off=qd)], [(0, 0, 0)], 1, _ep_plain, 4 * kvh * hd)
    win = _matmul([h], [_W(w_in, (j,), col_off=qd + 4 * kvh * hd)], [(0, 0, 0)], 1, _ep_plain, 2 * kvh * hd)
    if past is None:
        q = _matmul([h], [_W(w_in, (j,))], [(0, 0, 0)], 1, _ep_plain, qd, out_dtype=bf16)
        gate = _matmul([h], [_W(w["gate_grp"])], [(0, 0, 0)], 1, _ep_sigmoid, kvh * V7X_LANES)
        cmp = _compress(kv, w["wh"], w["wt"], batch=batch)
        o = _nsa_prompt(q, kv.astype(bf16), win.astype(bf16), cmp, gate, batch=batch, kvh=kvh, hd=hd,
                        window=window, cmp_len=w["cmp_len"])
        new_win = win.reshape(batch, t, 2, kvh, hd)[:, -window:]
    else:
        assert t == 1
        cache_kv, cache_win, page_table = past
        q = _matmul([h], [_W(w_in, (j,))], [(0, 0, 0)], 1, _ep_plain, qd)
        gate = _matmul([h], [_W(w["gate_flat"])], [(0, 0, 0)], 1, _ep_sigmoid, w["gate_flat"].shape[1])
        gate = gate[:, :3 * n_heads].reshape(batch, 3, n_heads).transpose(0, 2, 1)
        gate = jnp.pad(gate, ((0, 0), (0, 0), (0, V7X_LANES - 3)))
        pool = cache_kv.reshape(cache_kv.shape[:3] + (4 * kvh, hd))
        cwin = cache_win.reshape(cache_win.shape[:3] + (2 * kvh, hd))
        o = _nsa_decode(q.reshape(batch, n_heads, hd), kv.reshape(batch, 4 * kvh, hd), win.reshape(batch, 2 * kvh, hd),
                        gate, pool, cwin, page_table, w["wh"].reshape(-1, 2 * kvh, hd), w["wt"].reshape(-1, 2 * kvh, hd),
                        layer=j, kvh=kvh, hd=hd, window=window, cmp_len=w["cmp_len"])
        o = o.reshape(batch, qd).astype(bf16)
        new_win = jnp.concatenate([cache_win[j].astype(f32), win.reshape(batch, t, 2, kvh, hd)], axis=1)
        new_win = new_win[:, -min(window, new_win.shape[1]):]
    x = _matmul([o], [_W(dense["nsa_out"], (j,))], [(0, 0, 0)], 1, _ep_resid, x.shape[1], extras=[(x, "mn")])
    return x, kv.reshape(batch, t, 4, kvh, hd), new_win


def _trunk(x, p, prm, dense, ab, nsa, states, past, dims, batch):
    depth = prm["norm_g"].shape[0]
    kv_rows, win_new, conv_new, sconv_new, ssm_new = [], [], [], [], []
    for i in range(depth):
        j = i // 2
        g = prm["norm_g"][i]
        x = _ffn(x, g[0], dense, i, 0, dims)
        h = _rmsnorm(x, g[1])
        if i % 2 == 0:
            x, cb, scb, sh = _ab_mixer(x, h, dense, ab[j], prm, j, states[j], dims, batch)
            conv_new.append(cb)
            sconv_new.append(scb)
            ssm_new.append(sh)
        else:
            x, rows, wb = _nsa_mixer(x, h, dense, nsa[j], j, past, dims, batch)
            kv_rows.append(rows)
            win_new.append(wb)
        x = _ffn(x, g[2], dense, i, 1, dims)
        hn = _rmsnorm(x, g[3])
        pp = _matmul([p[i].astype(bf16)], [_W(dense["ple_proj"], (i,))], [(0, 0, 0)], 1, _ep_plain, x.shape[1])
        x = _matmul([hn], [_W(dense["ple_gate"], (i,))], [(0, 0, 0)], 1, _ep_ple, x.shape[1],
                    extras=[(x, "mn"), (pp, "mn")])
    y = _rmsnorm(x, prm["final_norm_g"], out_dtype=f32)
    return (y, jnp.stack(kv_rows), jnp.stack(win_new), jnp.stack(conv_new), jnp.stack(sconv_new),
            jnp.stack(ssm_new))


def kernel(x_prompt, x_sample, cache_nsa_kv, cache_nsa_win, state_conv, state_ssm_conv, state_ssm, page_table,
           p_prompt, p_sample, norm_g, final_norm_g, ffn_w_up, ffn_w_down, ple_w_gate, ple_w_proj, ab_w_in,
           ab_w_out, conv_dw_w, conv_dw_b, conv_ln_g, conv_ln_b, ssm_conv_w, ssm_conv_b, ssm_dt_bias, ssm_a_log,
           ssm_d, ssm_norm_g, nsa_w_in, nsa_w_out, nsa_cmp_w):
    bp, seq, d_model = x_prompt.shape
    bs, dec_seq, _ = x_sample.shape
    depth = norm_g.shape[0]
    n_ab, _, heads, p_dim, n_state = state_ssm.shape
    n_c = cache_nsa_kv.shape[0]
    kvh, hd = cache_nsa_kv.shape[-2:]
    d_inner = heads * p_dim
    xbc = state_ssm_conv.shape[-1]
    ffn = ffn_w_down.shape[2]
    dims = {
        "ffn": ffn,
        "conv_ch": state_conv.shape[-1], "d_inner": d_inner, "xbc": xbc,
        "groups": (xbc - d_inner) // (2 * n_state),
        "qd": nsa_w_out.shape[1], "kvh": kvh, "hd": hd, "n_heads": nsa_w_out.shape[1] // hd,
        "window": cache_nsa_win.shape[2],
    }
    prm = {
        "norm_g": norm_g, "final_norm_g": final_norm_g, "conv_dw_w": conv_dw_w, "conv_dw_b": conv_dw_b,
        "conv_ln_g": conv_ln_g, "conv_ln_b": conv_ln_b, "ssm_conv_w": ssm_conv_w, "ssm_conv_b": ssm_conv_b,
        "ssm_norm_g": ssm_norm_g,
    }
    dense = _prep_dense(dims, ffn_w_up, ffn_w_down, ple_w_gate, ple_w_proj, ab_w_in, ab_w_out, nsa_w_in, nsa_w_out)
    ab = [_prep_ab(j, dims, ab_w_in, ssm_dt_bias, ssm_a_log, ssm_d) for j in range(n_ab)]
    nsa = [_prep_nsa(j, dims, nsa_w_in, nsa_cmp_w) for j in range(n_c)]

    zero_states = [
        (jnp.zeros((bp,) + state_conv.shape[2:], f32), jnp.zeros((bp,) + state_ssm_conv.shape[2:], f32),
         jnp.zeros((bp, heads, p_dim, n_state), f32))
        for _ in range(n_ab)
    ]
    y_p, kv_p, win_p, conv_p, sconv_p, ssm_p = _trunk(
        x_prompt.reshape(bp * seq, d_model), p_prompt.reshape(depth, bp * seq, -1), prm, dense, ab, nsa,
        zero_states, None, dims, bp)

    assert dec_seq == 1
    states = [(state_conv[j], state_ssm_conv[j], state_ssm[j]) for j in range(n_ab)]
    y_s, kv_s, win_s, conv_s, sconv_s, ssm_s = _trunk(
        x_sample.reshape(bs * dec_seq, d_model), p_sample.reshape(depth, bs * dec_seq, -1), prm, dense, ab, nsa,
        states, (cache_nsa_kv, cache_nsa_win, page_table), dims, bs)

    return (y_p.reshape(bp, seq, d_model), y_s.reshape(bs, dec_seq, d_model), kv_p, kv_s, win_p, win_s,
            conv_p, conv_s, sconv_p, sconv_s, ssm_p, ssm_s)
```

```python
import functools
from typing import NamedTuple

import jax
import jax.numpy as jnp
from jax import lax
from jax.experimental import pallas as pl
from jax.experimental.pallas import tpu as pltpu

f32 = jnp.float32
bf16 = jnp.bfloat16

V7X_LANES = 128
V7X_SUBLANES = 8
V7X_VMEM_BYTES = 64 * 1024 * 1024
VMEM_BUDGET = 46 * 1024 * 1024
VMEM_LIMIT = 56 * 1024 * 1024

NORM_EPS = 1e-6
CMP_STRIDE = 16
SEL_BLOCK = 64
SEL_TOPK = 16
SSM_CHUNK = 128
NEG = -1e30
FORCE_SCORE = 1e6


def _cparams(sem):
    return pltpu.CompilerParams(dimension_semantics=sem, vmem_limit_bytes=VMEM_LIMIT)


def _sigmoid(x):
    return jax.nn.sigmoid(x)


def _softplus(x):
    return jnp.maximum(x, 0.0) + jnp.log(1.0 + jnp.exp(-jnp.abs(x)))


def _mm_kernel(*refs, n_x, n_w, n_extra, n_acc, terms, epilogue, nk):
    xs = refs[:n_x]
    ws = refs[n_x:n_x + n_w]
    extras = refs[n_x + n_w:n_x + n_w + n_extra]
    out = refs[n_x + n_w + n_extra]
    accs = refs[n_x + n_w + n_extra + 1:]

    def partials():
        res = [None] * n_acc
        for xi, wi, ai in terms:
            d = jnp.dot(xs[xi][...], ws[wi][...], preferred_element_type=f32)
            res[ai] = d if res[ai] is None else res[ai] + d
        return res

    if nk == 1:
        out[...] = epilogue(partials(), [e[...] for e in extras]).astype(out.dtype)
        return

    k = pl.program_id(2)
    parts = partials()

    @pl.when(k == 0)
    def _():
        for a, p in zip(accs, parts):
            a[...] = p

    @pl.when(jnp.logical_and(k > 0, k < nk - 1))
    def _():
        for a, p in zip(accs, parts):
            a[...] += p

    @pl.when(k == nk - 1)
    def _():
        full = [a[...] + p for a, p in zip(accs, parts)]
        out[...] = epilogue(full, [e[...] for e in extras]).astype(out.dtype)


class _X(NamedTuple):
    arr: jax.Array
    width: int | None = None
    col_off: int = 0


class _W(NamedTuple):
    arr: jax.Array
    lead: tuple = ()
    rows: int | None = None
    row_off: int = 0
    col_off: int = 0


def _matmul(xs, ws, terms, n_acc, epilogue, n_cols, *, extras=(), out_dtype=f32, nk=1, bn_cap=1024, bm_cap=1024):
    xs = [x if isinstance(x, _X) else _X(x) for x in xs]
    m = xs[0].arr.shape[0]
    bm = next((c for c in (bm_cap, 1024) if m % c == 0), m)
    bm = min(bm, m)
    assert m % bm == 0
    xk = [(x.width or x.arr.shape[1]) // nk for x in xs]
    wk = [(w.rows or w.arr.shape[-2]) // nk for w in ws]
    xbytes = sum(bm * k * x.arr.dtype.itemsize for x, k in zip(xs, xk))

    def est(bn):
        wbytes = sum(k * bn * w.arr.dtype.itemsize for w, k in zip(ws, wk))
        ebytes = sum((bm if kind == "mn" else 1) * bn * e.dtype.itemsize for e, kind in extras)
        obytes = bm * bn * jnp.dtype(out_dtype).itemsize
        abytes = n_acc * bm * bn * 4 * (2 if nk > 1 else 1)
        return 2 * (xbytes + wbytes + ebytes + obytes) + abytes

    cands = [c for c in (2048, 1024, 512, 256, 128)
             if c <= bn_cap and n_cols % c == 0 and all(w.col_off % c == 0 for w in ws)]
    bn = next((c for c in cands if est(c) <= VMEM_BUDGET), cands[-1])
    grid = (m // bm, n_cols // bn, nk)

    in_specs = []
    for x, bk in zip(xs, xk):
        assert x.col_off % bk == 0
        in_specs.append(pl.BlockSpec((bm, bk), lambda i, j, k, o=x.col_off // bk: (i, o + k)))
    for w, bk in zip(ws, wk):
        assert w.row_off % bk == 0
        in_specs.append(pl.BlockSpec(
            (None,) * len(w.lead) + (bk, bn),
            lambda i, j, k, lead=tuple(w.lead), ro=w.row_off // bk, co=w.col_off // bn: lead + (ro + k, co + j)))
    for e, kind in extras:
        if kind == "mn":
            in_specs.append(pl.BlockSpec((bm, bn), lambda i, j, k: (i, j)))
        else:
            in_specs.append(pl.BlockSpec((1, bn), lambda i, j, k: (0, j)))
    scratch = [pltpu.VMEM((bm, bn), f32) for _ in range(n_acc)] if nk > 1 else []
    kern = functools.partial(_mm_kernel, n_x=len(xs), n_w=len(ws), n_extra=len(extras), n_acc=n_acc,
                             terms=tuple(terms), epilogue=epilogue, nk=nk)
    return pl.pallas_call(
        kern,
        out_shape=jax.ShapeDtypeStruct((m, n_cols), out_dtype),
        grid=grid,
        in_specs=in_specs,
        out_specs=pl.BlockSpec((bm, bn), lambda i, j, k: (i, j)),
        scratch_shapes=scratch,
        compiler_params=_cparams(("parallel", "parallel", "arbitrary")),
        name="mm",
    )(*[x.arr for x in xs], *[w.arr for w in ws], *[e for e, _ in extras])


def _ep_plain(accs, extras):
    return accs[0]


def _ep_swiglu(accs, extras):
    a, b = accs
    return a * _sigmoid(a) * b


def _ep_glu(accs, extras):
    return accs[0] * _sigmoid(accs[1])


def _ep_sigmoid(accs, extras):
    return _sigmoid(accs[0])


def _ep_softplus_bias(accs, extras):
    return _softplus(accs[0] + extras[0])


def _ep_half_resid(accs, extras):
    return extras[0] + 0.5 * accs[0]


def _ep_resid(accs, extras):
    return extras[0] + accs[0]


def _ep_ple(accs, extras):
    return extras[0] + _sigmoid(accs[0]) * accs[1]


def _rmsnorm_kernel(x_ref, g_ref, o_ref):
    x = x_ref[...]
    y = x * lax.rsqrt(jnp.mean(x * x, axis=-1, keepdims=True) + NORM_EPS)
    o_ref[...] = (y * g_ref[...]).astype(o_ref.dtype)


def _ln_silu_kernel(x_ref, g_ref, b_ref, o_ref):
    x = x_ref[...]
    xc = x - jnp.mean(x, axis=-1, keepdims=True)
    var = jnp.mean(xc * xc, axis=-1, keepdims=True)
    y = xc * lax.rsqrt(var + NORM_EPS) * g_ref[...] + b_ref[...]
    o_ref[...] = (y * _sigmoid(y)).astype(o_ref.dtype)


def _rowwise(kern, x, vecs, out_dtype, name):
    m, d = x.shape
    bm = m if m <= 256 else 256
    assert m % bm == 0
    return pl.pallas_call(
        kern,
        out_shape=jax.ShapeDtypeStruct((m, d), out_dtype),
        grid=(m // bm,),
        in_specs=[pl.BlockSpec((bm, d), lambda i: (i, 0))] + [pl.BlockSpec((1, d), lambda i: (0, 0)) for _ in vecs],
        out_specs=pl.BlockSpec((bm, d), lambda i: (i, 0)),
        compiler_params=_cparams(("parallel",)),
        name=name,
    )(x, *[v.reshape(1, d).astype(f32) for v in vecs])


def _rmsnorm(x, g, out_dtype=bf16):
    return _rowwise(_rmsnorm_kernel, x, [g], out_dtype, "rmsnorm")


def _ln_silu(x, g, b):
    return _rowwise(_ln_silu_kernel, x, [g, b], bf16, "ln_silu")


def _conv_kernel(u_ref, st_ref, w_ref, b_ref, o_ref, win_ref, *shifted, taps, halo, tt, silu):
    j = pl.program_id(2)
    sub = V7X_SUBLANES

    @pl.when(j == 0)
    def _():
        win_ref[0:halo, :] = st_ref[0]

    win_ref[halo:halo + tt, :] = u_ref[...]
    off = halo - (taps - 1)
    bias = b_ref[...]
    if shifted:
        sh_ref = shifted[0]
        span = halo + tt - sub
        for s in range(1, sub):
            sh_ref[s - 1] = win_ref[s:s + span, :]

        def tap(r0, k):
            a, s = divmod(off + k, sub)
            lo = r0 + sub * a
            return win_ref[lo:lo + sub, :] if s == 0 else sh_ref[s - 1, lo:lo + sub, :]
    else:
        def tap(r0, k):
            return win_ref[r0 + off + k:r0 + off + k + sub, :]

    for r in range(tt // V7X_SUBLANES):
        r0 = r * V7X_SUBLANES
        acc = jnp.broadcast_to(bias, (V7X_SUBLANES, bias.shape[1]))
        for k in range(taps):
            acc = acc + w_ref[k * sub:(k + 1) * sub, :] * tap(r0, k)
        if silu:
            acc = acc * _sigmoid(acc)
        o_ref[r0:r0 + V7X_SUBLANES, :] = acc.astype(o_ref.dtype)
    win_ref[0:halo, :] = win_ref[tt:tt + halo, :]


def _causal_conv(u, state, w, b, *, batch, silu, out_dtype=f32):
    m, c = u.shape
    t = m // batch
    taps = w.shape[0]
    halo = -(-(taps - 1) // V7X_SUBLANES) * V7X_SUBLANES
    use_shifted = taps > V7X_SUBLANES
    tt = min(t, 256 if use_shifted else 1024)
    assert t % tt == 0 and tt % V7X_SUBLANES == 0
    cc = 512 if c % 512 == 0 else V7X_LANES
    st = jnp.pad(state.astype(f32), ((0, 0), (halo - (taps - 1), 0), (0, 0)))
    nt = t // tt
    kern = functools.partial(_conv_kernel, taps=taps, halo=halo, tt=tt, silu=silu)
    scratch = [pltpu.VMEM((halo + tt, cc), f32)]
    if use_shifted:
        scratch.append(pltpu.VMEM((V7X_SUBLANES - 1, halo + tt - V7X_SUBLANES, cc), f32))
    return pl.pallas_call(
        kern,
        out_shape=jax.ShapeDtypeStruct((m, c), out_dtype),
        grid=(batch, c // cc, nt),
        in_specs=[
            pl.BlockSpec((tt, cc), lambda bi, ci, j: (bi * nt + j, ci)),
            pl.BlockSpec((1, halo, cc), lambda bi, ci, j: (bi, 0, ci)),
            pl.BlockSpec((taps * V7X_SUBLANES, cc), lambda bi, ci, j: (0, ci)),
            pl.BlockSpec((1, cc), lambda bi, ci, j: (0, ci)),
        ],
        out_specs=pl.BlockSpec((tt, cc), lambda bi, ci, j: (bi * nt + j, ci)),
        scratch_shapes=scratch,
        compiler_params=_cparams(("parallel", "parallel", "arbitrary")),
        name="causal_conv",
    )(u, st, jnp.repeat(w.astype(f32), V7X_SUBLANES, axis=0), b.reshape(1, c).astype(f32))


def _split3(x):
    hi = x.astype(bf16)
    r1 = x - hi.astype(f32)
    mid = r1.astype(bf16)
    lo = (r1 - mid.astype(f32)).astype(bf16)
    return hi, mid, lo


def _dot_nt(a, b):
    return lax.dot_general(a, b, (((1,), (1,)), ((), ())), preferred_element_type=f32)


def _dot_tn(a, b):
    return lax.dot_general(a, b, (((0,), (0,)), ((), ())), preferred_element_type=f32)


def _ssd_kernel(xs_ref, b_ref, c_ref, dt_ref, z_ref, a_ref, d_ref, ng_ref, h0_ref, y_ref, hout_ref, h_scr,
                *, q, e_heads, p_dim, nc):
    c = pl.program_id(2)

    n = b_ref.shape[1]
    gw = e_heads * p_dim
    hb = V7X_LANES // p_dim

    @pl.when(c == 0)
    def _():
        h_scr[...] = h0_ref[0].reshape(gw, n)

    xs = xs_ref[...]
    bb = b_ref[...].astype(bf16)
    cb16 = c_ref[...].astype(bf16)
    dt = dt_ref[...]
    la = dt * a_ref[0]
    ri = lax.broadcasted_iota(jnp.int32, (q, q), 0)
    ci = lax.broadcasted_iota(jnp.int32, (q, q), 1)
    causal = ri >= ci
    tri = jnp.where(causal, 1.0, 0.0).astype(bf16)

    def dot3(v, mat):
        hi, mid, lo = _split3(v)
        return (jnp.dot(hi, mat, preferred_element_type=f32) + jnp.dot(mid, mat, preferred_element_type=f32)
                + jnp.dot(lo, mat, preferred_element_type=f32))

    def spread(width):
        src = lax.broadcasted_iota(jnp.int32, (V7X_LANES, e_heads * width), 0)
        dst = lax.shift_right_logical(lax.broadcasted_iota(jnp.int32, (V7X_LANES, e_heads * width), 1), _log2(width))
        return jnp.where(src == dst, 1.0, 0.0).astype(bf16)

    hi, mid, lo = _split3(la)
    cs = (jnp.dot(tri, hi, preferred_element_type=f32) + jnp.dot(tri, mid, preferred_element_type=f32)
          + jnp.dot(tri, lo, preferred_element_type=f32))
    cs_t = cs.T
    to_p = spread(p_dim)
    to_q = spread(q)
    dt_p = dot3(dt, to_p)
    cs_p = dot3(cs, to_p)
    cs_q = dot3(cs, to_q)
    d_p = dot3(jnp.broadcast_to(d_ref[0], (V7X_SUBLANES, V7X_LANES)), to_p)[0:1]
    xd = xs * dt_p
    xd16 = xd.astype(bf16)
    xw16 = (xd * jnp.exp(cs_p[q - 1:q, :] - cs_p)).astype(bf16)
    grow = jnp.exp(cs_p)
    cb = _dot_nt(cb16, bb)
    lane_head = lax.shift_right_logical(lax.broadcasted_iota(jnp.int32, (q, V7X_LANES), 1), _log2(p_dim))
    ys = []
    for pr in range(e_heads // hb):
        blk = slice(pr * V7X_LANES, (pr + 1) * V7X_LANES)
        x2 = xd16[:, blk]
        yd = None
        keep = []
        for k in range(hb):
            e = pr * hb + k
            col = cs_q[:, e * q:(e + 1) * q]
            decay = jnp.exp(jnp.where(causal, col - cs_t[e:e + 1, :], NEG))
            yk = jnp.dot((cb * decay).astype(bf16), x2, preferred_element_type=f32)
            yd = yk if yd is None else jnp.where(lane_head == k, yk, yd)
            keep.append(jnp.broadcast_to(jnp.exp(col[q - 1:q, :]), (p_dim, n)))
        h2 = h_scr[blk, :]
        s2 = _dot_tn(xw16[:, blk], bb)
        ys.append(yd + _dot_nt(cb16, h2.astype(bf16)) * grow[:, blk])
        h_scr[blk, :] = h2 * jnp.concatenate(keep, axis=0) + s2
    y = jnp.concatenate(ys, axis=1) + d_p * xs
    z = z_ref[...]
    yg = y * (z * _sigmoid(z))
    yg = yg * lax.rsqrt(jnp.mean(yg * yg, axis=-1, keepdims=True) + NORM_EPS)
    y_ref[...] = (yg * ng_ref[...]).astype(y_ref.dtype)

    @pl.when(c == nc - 1)
    def _():
        hout_ref[0] = h_scr[...].reshape(e_heads, p_dim, n)


def _ssd(xbc, dtg, z, a_g, d_g, norm_g, h0, *, batch, groups):
    m = xbc.shape[0]
    t = m // batch
    _, heads, p_dim, n = h0.shape
    d_inner = heads * p_dim
    e_heads = heads // groups
    gw = e_heads * p_dim
    q = SSM_CHUNK
    assert t % q == 0 and gw % V7X_LANES == 0 and n == q and V7X_LANES % p_dim == 0
    assert e_heads % (V7X_LANES // p_dim) == 0 and e_heads <= V7X_LANES
    nc = t // q
    b_off = d_inner // n
    c_off = b_off + groups
    kern = functools.partial(_ssd_kernel, q=q, e_heads=e_heads, p_dim=p_dim, nc=nc)
    return pl.pallas_call(
        kern,
        out_shape=(jax.ShapeDtypeStruct((m, d_inner), bf16), jax.ShapeDtypeStruct((batch, heads, p_dim, n), f32)),
        grid=(batch, groups, nc),
        in_specs=[
            pl.BlockSpec((q, gw), lambda bi, g, c: (bi * nc + c, g)),
            pl.BlockSpec((q, n), lambda bi, g, c: (bi * nc + c, b_off + g)),
            pl.BlockSpec((q, n), lambda bi, g, c: (bi * nc + c, c_off + g)),
            pl.BlockSpec((q, V7X_LANES), lambda bi, g, c: (bi * nc + c, g)),
            pl.BlockSpec((q, gw), lambda bi, g, c: (bi * nc + c, g)),
            pl.BlockSpec((1, 1, V7X_LANES), lambda bi, g, c: (g, 0, 0)),
            pl.BlockSpec((1, 1, V7X_LANES), lambda bi, g, c: (g, 0, 0)),
            pl.BlockSpec((1, gw), lambda bi, g, c: (0, g)),
            pl.BlockSpec((1, e_heads, p_dim, n), lambda bi, g, c: (bi, g, 0, 0)),
        ],
        out_specs=(
            pl.BlockSpec((q, gw), lambda bi, g, c: (bi * nc + c, g)),
            pl.BlockSpec((1, e_heads, p_dim, n), lambda bi, g, c: (bi, g, 0, 0)),
        ),
        scratch_shapes=[pltpu.VMEM((gw, n), f32)],
        compiler_params=_cparams(("parallel", "parallel", "arbitrary")),
        name="ssd",
    )(xbc, xbc, xbc, dtg, z, a_g, d_g, norm_g.reshape(1, d_inner).astype(f32), h0)


def _log2(v):
    assert v > 0 and v & (v - 1) == 0, v
    return v.bit_length() - 1


def _masked_softmax(s, mask):
    s = jnp.where(mask, s, NEG)
    m = jnp.max(s, axis=-1, keepdims=True)
    e = jnp.where(mask, jnp.exp(s - m), 0.0)
    return e / jnp.maximum(jnp.sum(e, axis=-1, keepdims=True), 1e-30)


def _compress_partial(rows, wh, wt, stride):
    t, c = rows.shape
    r = rows.reshape(t // stride, stride, c)
    return jnp.sum(r * wh[None], axis=1), jnp.sum(r * wt[None], axis=1)


def _compress_combine(head, tail):
    n, c = head.shape
    nxt = pltpu.roll(tail, n - 1, 0)
    keep = lax.broadcasted_iota(jnp.int32, (n, c), 0) < n - 1
    return head + jnp.where(keep, nxt, 0.0)


def _compress_kernel(rows_ref, wh_ref, wt_ref, o_ref, *, stride):
    head, tail = _compress_partial(rows_ref[...], wh_ref[...], wt_ref[...], stride)
    o_ref[0] = _compress_combine(head, tail)


def _compress(kv, wh, wt, *, batch):
    m = kv.shape[0]
    t = m // batch
    stride, c2 = wh.shape
    cc = 256
    assert c2 % cc == 0 and t % stride == 0
    return pl.pallas_call(
        functools.partial(_compress_kernel, stride=stride),
        out_shape=jax.ShapeDtypeStruct((batch, t // stride, c2), f32),
        grid=(batch, c2 // cc),
        in_specs=[
            pl.BlockSpec((t, cc), lambda bi, ci: (bi, ci)),
            pl.BlockSpec((stride, cc), lambda bi, ci: (0, ci)),
            pl.BlockSpec((stride, cc), lambda bi, ci: (0, ci)),
        ],
        out_specs=pl.BlockSpec((1, t // stride, cc), lambda bi, ci: (bi, 0, ci)),
        compiler_params=_cparams(("parallel", "parallel")),
        name="nsa_compress",
    )(kv, wh, wt)


_LOG2E = 1.4426950408889634


def _nsa_prompt_kernel(q_ref, ks_ref, vs_ref, kw_ref, vw_ref, kc_ref, vc_ref, gate_ref, o_ref, tm_scr,
                       *, tq, e_heads, hd, n_heads, t, window, cmp_len, tk_sel):
    g = pl.program_id(1)
    i = pl.program_id(2)
    q0 = i * tq
    rows = e_heads * tq
    scale = hd ** -0.5
    q = q_ref[...]
    qs = jnp.concatenate([q[:, e * hd:(e + 1) * hd] for e in range(e_heads)], axis=0)
    r_idx = lax.broadcasted_iota(jnp.int32, (rows, 1), 0)
    e_idx = lax.shift_right_logical(r_idx, _log2(tq))
    t_idx = r_idx - e_idx * tq
    slope = jnp.exp2(-8.0 * (g * e_heads + e_idx + 1).astype(f32) / n_heads)
    qpos = q0 + t_idx

    nc = kc_ref.shape[1]
    kc = kc_ref[0].astype(bf16)
    vc = vc_ref[0].astype(bf16)
    s = _dot_nt(qs, kc) * scale
    c_end = lax.broadcasted_iota(jnp.int32, (1, nc), 1) * CMP_STRIDE + (cmp_len - 1)
    dist = qpos - c_end
    s = s - slope * dist.astype(f32)
    p_c = _masked_softmax(s, dist >= 0)
    o_c = jnp.dot(p_c.astype(bf16), vc, preferred_element_type=f32)
    imp = jnp.sum(p_c.reshape(e_heads, tq, nc), axis=0)

    n_sel = -(-t // SEL_BLOCK)
    k_top = min(SEL_TOPK, n_sel)
    nsp = -(-n_sel // V7X_SUBLANES) * V7X_SUBLANES
    shift = _log2(SEL_BLOCK // CMP_STRIDE)
    gj = lax.broadcasted_iota(jnp.int32, (nsp, nc), 0)
    gi = lax.shift_right_logical(lax.broadcasted_iota(jnp.int32, (nsp, nc), 1), shift)
    gm_t = jnp.where(gi == gj, 1.0, 0.0).astype(bf16)
    hi, mid, lo = _split3(imp)
    bimp_t = _dot_nt(gm_t, hi) + _dot_nt(gm_t, mid) + _dot_nt(gm_t, lo)
    jl = lax.broadcasted_iota(jnp.int32, (nsp, tq), 0)
    tpos = q0 + lax.broadcasted_iota(jnp.int32, (nsp, tq), 1)
    cur = lax.shift_right_logical(tpos, _log2(SEL_BLOCK))
    valid = jl * SEL_BLOCK <= tpos
    forced = (jl == 0) | (jl == cur) | (jl == cur - 1)
    score = jnp.where(valid, jnp.where(forced, FORCE_SCORE, bimp_t), -FORCE_SCORE)
    rank = jnp.zeros((nsp, tq), f32)
    for j2 in range(n_sel):
        other = score[j2:j2 + 1, :]
        beats = (other > score) | ((other == score) & (jl > j2))
        rank = rank + jnp.where(beats, 1.0, 0.0)
    sel_t = jnp.where(rank < k_top, 1.0, 0.0).astype(bf16)

    wk = tm_scr.shape[1]

    @pl.when(i == 0)
    def _():
        rel = (lax.broadcasted_iota(jnp.int32, (rows, wk), 1) - t_idx).astype(f32)
        tm_scr[...] = slope * (rel * _LOG2E)

    c_scale = scale * _LOG2E

    def sel_body(kt, carry):
        m, l, acc = carry
        k0 = pl.multiple_of(kt * tk_sel, tk_sel)
        kb = ks_ref[pl.ds(k0, tk_sel), :]
        vb = vs_ref[pl.ds(k0, tk_sel), :]
        s2 = _dot_nt(qs, kb) * c_scale + tm_scr[:, 0:tk_sel]
        jj = lax.broadcasted_iota(jnp.int32, (nsp, tk_sel), 0)
        kk = lax.shift_right_logical(k0 + lax.broadcasted_iota(jnp.int32, (nsp, tk_sel), 1), _log2(SEL_BLOCK))
        expand = jnp.where(jj == kk, 1.0, 0.0).astype(bf16)
        ms = _dot_tn(sel_t, expand)
        tt_ = q0 + lax.broadcasted_iota(jnp.int32, (tq, tk_sel), 0)
        cc_ = k0 + lax.broadcasted_iota(jnp.int32, (tq, tk_sel), 1)
        madd = jnp.where((ms > 0.5) & (cc_ <= tt_), 0.0, NEG)
        s3 = (s2.reshape(e_heads, tq, tk_sel) + madd[None]).reshape(rows, tk_sel)
        rowoff = slope * ((k0 - q0).astype(f32) * _LOG2E)
        m_new = jnp.maximum(m, jnp.max(s3, axis=-1, keepdims=True) + rowoff)
        p = jnp.exp2(s3 - (m_new - rowoff))
        alpha = jnp.exp2(m - m_new)
        l = alpha * l + jnp.sum(p, axis=-1, keepdims=True)
        acc = alpha * acc + jnp.dot(p.astype(bf16), vb, preferred_element_type=f32)
        return m_new, l, acc

    init = (jnp.full((rows, 1), NEG, f32), jnp.zeros((rows, 1), f32), jnp.zeros((rows, hd), f32))
    hi_sel = (q0 + tq + tk_sel - 1) // tk_sel
    _, l_s, acc_s = lax.fori_loop(0, hi_sel, sel_body, init)
    o_s = acc_s / jnp.maximum(l_s, 1e-30)

    start = pl.multiple_of(jnp.clip(q0 - window, 0, t - wk), tq)
    kb = kw_ref[pl.ds(start, wk), :]
    vb = vw_ref[pl.ds(start, wk), :]
    s2 = _dot_nt(qs, kb) * c_scale + tm_scr[...]
    dw = (q0 + lax.broadcasted_iota(jnp.int32, (tq, wk), 0)) - (start + lax.broadcasted_iota(jnp.int32, (tq, wk), 1))
    madd = jnp.where((dw >= 0) & (dw < window), 0.0, NEG)
    s3 = (s2.reshape(e_heads, tq, wk) + madd[None]).reshape(rows, wk)
    p = jnp.exp2(s3 - jnp.max(s3, axis=-1, keepdims=True))
    o_w = (jnp.dot(p.astype(bf16), vb, preferred_element_type=f32)
           / jnp.maximum(jnp.sum(p, axis=-1, keepdims=True), 1e-30))

    gates = gate_ref[...]
    outs = []
    for e in range(e_heads):
        sl = slice(e * tq, (e + 1) * tq)
        outs.append(gates[:, e:e + 1] * o_c[sl] + gates[:, e_heads + e:e_heads + e + 1] * o_s[sl]
                    + gates[:, 2 * e_heads + e:2 * e_heads + e + 1] * o_w[sl])
    o_ref[...] = jnp.concatenate(outs, axis=1).astype(o_ref.dtype)


def _nsa_prompt(q, kvb, winb, cmp, gate, *, batch, kvh, hd, window, cmp_len):
    m, qd = q.shape
    t = m // batch
    n_heads = qd // hd
    e_heads = n_heads // kvh
    tq = 128
    tk_sel = 256
    wk = window + tq
    assert t % tq == 0 and t % tk_sel == 0 and window % tq == 0 and t >= wk and wk >= tk_sel
    nq = t // tq
    nc = cmp.shape[1]
    gw = e_heads * hd
    kern = functools.partial(_nsa_prompt_kernel, tq=tq, e_heads=e_heads, hd=hd, n_heads=n_heads, t=t, window=window,
                             cmp_len=cmp_len, tk_sel=tk_sel)
    return pl.pallas_call(
        kern,
        out_shape=jax.ShapeDtypeStruct((m, qd), bf16),
        grid=(batch, kvh, nq),
        in_specs=[
            pl.BlockSpec((tq, gw), lambda bi, g, i: (bi * nq + i, g)),
            pl.BlockSpec((t, hd), lambda bi, g, i: (bi, 2 * kvh + g)),
            pl.BlockSpec((t, hd), lambda bi, g, i: (bi, 3 * kvh + g)),
            pl.BlockSpec((t, hd), lambda bi, g, i: (bi, g)),
            pl.BlockSpec((t, hd), lambda bi, g, i: (bi, kvh + g)),
            pl.BlockSpec((1, nc, hd), lambda bi, g, i: (bi, 0, g)),
            pl.BlockSpec((1, nc, hd), lambda bi, g, i: (bi, 0, kvh + g)),
            pl.BlockSpec((tq, V7X_LANES), lambda bi, g, i: (bi * nq + i, g)),
        ],
        out_specs=pl.BlockSpec((tq, gw), lambda bi, g, i: (bi * nq + i, g)),
        scratch_shapes=[pltpu.VMEM((e_heads * tq, wk), f32)],
        compiler_params=_cparams(("parallel", "parallel", "arbitrary")),
        name="nsa_prompt",
    )(q, kvb, kvb, winb, winb, cmp, cmp, gate)


def _nsa_decode_kernel(pt_ref, *refs, pps, kvh, e_heads, hd, n_heads, past, window, cmp_len, n_steps):
    page_refs = refs[:pps]
    q_ref, kvn_ref, wn_ref, gate_ref, cwin_ref, wh_ref, wt_ref, o_ref = refs[pps:pps + 8]
    head_scr, tail_scr, sel_scr, m_scr, l_scr, acc_scr, oc_scr = refs[pps + 8:]
    ps = pl.program_id(1)
    st = pl.program_id(2)
    scale = hd ** -0.5
    page_rows = page_refs[0].shape[2]
    c_rows = 2 * kvh
    c_shift = _log2(c_rows)
    sub = page_rows // CMP_STRIDE
    n_sub = past // CMP_STRIDE
    n_sel = past // SEL_BLOCK + 1
    k_top = min(SEL_TOPK, n_sel)
    n_sel_pad = sel_scr.shape[1]
    cur = past // SEL_BLOCK

    h_idx = lax.broadcasted_iota(jnp.int32, (n_heads, 1), 0)
    g_idx = lax.shift_right_logical(h_idx, _log2(e_heads))
    slope = jnp.exp2(-8.0 * (h_idx + 1).astype(f32) / n_heads)

    def rnd(x):
        return x.astype(bf16).astype(f32)

    def per_head(rows):
        return jnp.concatenate([jnp.broadcast_to(rows[g:g + 1], (e_heads, hd)) for g in range(kvh)], axis=0)

    def own_key_columns(n_cols):
        col = lax.broadcasted_iota(jnp.int32, (1, n_cols), 1)
        return lax.shift_right_logical(col, c_shift), (col & (c_rows - 1)) == g_idx

    @pl.when(ps == 0)
    def _():
        wh = wh_ref[...]
        wt = wt_ref[...]
        for r in range(pps):
            row0 = (st * pps + r) * sub
            blk = page_refs[r][0, 0].reshape(sub, CMP_STRIDE, c_rows, hd)
            head_scr[pl.ds(row0, sub)] = jnp.sum(blk * wh[None], axis=1)
            tail_scr[pl.ds(row0, sub)] = jnp.sum(blk * wt[None], axis=1)

    @pl.when(jnp.logical_and(ps == 1, st == 0))
    def _():
        tail = tail_scr[...]
        cmp = head_scr[...] + jnp.concatenate([tail[1:], jnp.zeros((1, c_rows, hd), f32)], axis=0)
        cmp2 = cmp.reshape(n_sub * c_rows, hd).astype(bf16)
        qb = q_ref[0].astype(bf16)
        i_idx, own = own_key_columns(n_sub * c_rows)
        dist = past - (i_idx * CMP_STRIDE + (cmp_len - 1))
        s = _dot_nt(qb, cmp2) * scale - slope * dist.astype(f32)
        p = _masked_softmax(s, own & (dist >= 0))
        oc_scr[...] = jnp.dot(pltpu.roll(p, kvh, 1).astype(bf16), cmp2, preferred_element_type=f32)
        imp = jnp.sum(p.reshape(kvh, e_heads, n_sub * c_rows), axis=1)
        imp = jnp.concatenate([imp, jnp.zeros((V7X_SUBLANES - kvh, n_sub * c_rows), f32)], axis=0)
        shift = _log2(SEL_BLOCK // CMP_STRIDE) + c_shift
        ii = lax.broadcasted_iota(jnp.int32, (n_sub * c_rows, n_sel_pad), 0)
        jj = lax.broadcasted_iota(jnp.int32, (n_sub * c_rows, n_sel_pad), 1)
        gm = jnp.where(lax.shift_right_logical(ii, shift) == jj, 1.0, 0.0).astype(bf16)
        hi, mid, lo = _split3(imp)
        bimp = (jnp.dot(hi, gm, preferred_element_type=f32) + jnp.dot(mid, gm, preferred_element_type=f32)
                + jnp.dot(lo, gm, preferred_element_type=f32))
        jl = lax.broadcasted_iota(jnp.int32, (V7X_SUBLANES, n_sel_pad), 1)
        valid = jl * SEL_BLOCK <= past
        forced = (jl == 0) | (jl == cur) | (jl == cur - 1)
        score = jnp.where(valid, jnp.where(forced, FORCE_SCORE, bimp), -FORCE_SCORE)
        score = jnp.where(jl < n_sel, score, -2.0 * FORCE_SCORE)
        j2 = lax.broadcasted_iota(jnp.int32, (n_sel_pad, n_sel_pad), 0)
        j1 = lax.broadcasted_iota(jnp.int32, (n_sel_pad, n_sel_pad), 1)
        for g in range(kvh):
            sc_row = jnp.broadcast_to(score[g:g + 1], (n_sel_pad, n_sel_pad))
            sc_col = sc_row.T
            beats = ((sc_col > sc_row) | ((sc_col == sc_row) & (j2 < j1))) & (j2 < n_sel)
            rank = jnp.sum(jnp.where(beats, 1.0, 0.0), axis=0, keepdims=True)
            sel_scr[g * e_heads:(g + 1) * e_heads, :] = jnp.broadcast_to(jnp.where(rank < k_top, 1.0, 0.0),
                                                                          (e_heads, n_sel_pad))
        m_scr[...] = jnp.full(m_scr.shape, NEG, f32)
        l_scr[...] = jnp.zeros(l_scr.shape, f32)
        acc_scr[...] = jnp.zeros(acc_scr.shape, f32)

    @pl.when(ps == 1)
    def _():
        qb = q_ref[0].astype(bf16)
        sel = sel_scr[...]
        jl = lax.broadcasted_iota(jnp.int32, (1, n_sel_pad), 1)
        n_cols = page_rows * c_rows
        r_idx, own = own_key_columns(n_cols)
        r_blk = lax.shift_right_logical(r_idx, _log2(SEL_BLOCK))
        for r in range(pps):
            blk = page_refs[r][0, 0].reshape(n_cols, hd).astype(bf16)
            k0 = (st * pps + r) * page_rows
            d = past - (k0 + r_idx)
            blk0 = (st * pps + r) * (page_rows // SEL_BLOCK)
            picked = jnp.zeros((n_heads, n_cols), f32)
            for b in range(page_rows // SEL_BLOCK):
                flag = jnp.sum(jnp.where(jl == blk0 + b, sel, 0.0), axis=-1, keepdims=True)
                picked = jnp.where(r_blk == b, flag, picked)
            mask = own & (picked > 0.5) & (d >= 0)
            sc = jnp.where(mask, _dot_nt(qb, blk) * scale - slope * d.astype(f32), NEG)
            m = m_scr[...]
            m_new = jnp.maximum(m, jnp.max(sc, axis=-1, keepdims=True))
            alpha = jnp.exp(m - m_new)
            p = jnp.where(mask, jnp.exp(sc - m_new), 0.0)
            l_scr[...] = alpha * l_scr[...] + jnp.sum(p, axis=-1, keepdims=True)
            acc_scr[...] = alpha * acc_scr[...] + jnp.dot(pltpu.roll(p, kvh, 1).astype(bf16), blk,
                                                          preferred_element_type=f32)
            m_scr[...] = m_new

    @pl.when(jnp.logical_and(ps == 1, st == n_steps - 1))
    def _():
        qb = q_ref[0].astype(bf16)
        q32 = qb.astype(f32)
        kn = per_head(rnd(kvn_ref[0, 2 * kvh:3 * kvh, :]))
        vn = per_head(rnd(kvn_ref[0, 3 * kvh:4 * kvh, :]))
        sel_new = sel_scr[:, n_sel - 1:n_sel] > 0.5
        s_n = jnp.where(sel_new, jnp.sum(q32 * kn, axis=-1, keepdims=True) * scale, NEG)
        m = m_scr[...]
        m_new = jnp.maximum(m, s_n)
        alpha = jnp.exp(m - m_new)
        p_n = jnp.where(sel_new, jnp.exp(s_n - m_new), 0.0)
        l = alpha * l_scr[...] + p_n
        o_s = (alpha * acc_scr[...] + rnd(p_n) * vn) / jnp.maximum(l, 1e-30)
        pb = cwin_ref.shape[2]
        cw = cwin_ref[0, 0].reshape(pb * c_rows, hd).astype(bf16)
        r_idx, own = own_key_columns(pb * c_rows)
        wpos = past - pb + r_idx
        dw = past - wpos
        wmask = own & (dw >= 0) & (dw < window) & (wpos >= 0)
        sw = jnp.where(wmask, _dot_nt(qb, cw) * scale - slope * dw.astype(f32), NEG)
        kwn = per_head(rnd(wn_ref[0, 0:kvh, :]))
        vwn = per_head(rnd(wn_ref[0, kvh:2 * kvh, :]))
        sw_n = jnp.sum(q32 * kwn, axis=-1, keepdims=True) * scale
        mw = jnp.maximum(jnp.max(sw, axis=-1, keepdims=True), sw_n)
        ew = jnp.where(wmask, jnp.exp(sw - mw), 0.0)
        en = jnp.exp(sw_n - mw)
        den = jnp.maximum(jnp.sum(ew, axis=-1, keepdims=True) + en, 1e-30)
        o_w = (jnp.dot(pltpu.roll(ew, kvh, 1).astype(bf16), cw, preferred_element_type=f32) + rnd(en) * vwn) / den
        gates = gate_ref[0]
        o_ref[0] = gates[:, 0:1] * oc_scr[...] + gates[:, 1:2] * o_s + gates[:, 2:3] * o_w


def _nsa_decode(q, kvn, wn, gate, pool, cwin, page_table, wh, wt, *, layer, kvh, hd, window, cmp_len):
    batch, n_heads, _ = q.shape
    e_heads = n_heads // kvh
    n_pages = page_table.shape[1]
    page_rows = pool.shape[2]
    past = n_pages * page_rows
    c_rows = 2 * kvh
    assert page_rows % SEL_BLOCK == 0 and page_rows % CMP_STRIDE == 0 and c_rows == V7X_SUBLANES
    pps = next(c for c in (8, 4, 2, 1) if n_pages % c == 0)
    n_steps = n_pages // pps
    n_sub = past // CMP_STRIDE
    n_sel_pad = -(-(past // SEL_BLOCK + 1) // V7X_LANES) * V7X_LANES
    pb = cwin.shape[2]
    kern = functools.partial(_nsa_decode_kernel, pps=pps, kvh=kvh, e_heads=e_heads, hd=hd, n_heads=n_heads, past=past,
                             window=window, cmp_len=cmp_len, n_steps=n_steps)
    page_specs = [
        pl.BlockSpec((1, 1, page_rows, c_rows, hd),
                     lambda b, ps, st, pt, r=r: (layer, pt[b, st * pps + r], 0, ps, 0))
        for r in range(pps)
    ]
    grid_spec = pltpu.PrefetchScalarGridSpec(
        num_scalar_prefetch=1,
        grid=(batch, 2, n_steps),
        in_specs=page_specs + [
            pl.BlockSpec((1, n_heads, hd), lambda b, ps, st, pt: (b, 0, 0)),
            pl.BlockSpec((1, 4 * kvh, hd), lambda b, ps, st, pt: (b, 0, 0)),
            pl.BlockSpec((1, 2 * kvh, hd), lambda b, ps, st, pt: (b, 0, 0)),
            pl.BlockSpec((1, n_heads, V7X_LANES), lambda b, ps, st, pt: (b, 0, 0)),
            pl.BlockSpec((1, 1, pb, c_rows, hd), lambda b, ps, st, pt: (layer, b, 0, 0, 0)),
            pl.BlockSpec((CMP_STRIDE, c_rows, hd), lambda b, ps, st, pt: (0, 0, 0)),
            pl.BlockSpec((CMP_STRIDE, c_rows, hd), lambda b, ps, st, pt: (0, 0, 0)),
        ],
        out_specs=pl.BlockSpec((1, n_heads, hd), lambda b, ps, st, pt: (b, 0, 0)),
        scratch_shapes=[
            pltpu.VMEM((n_sub, c_rows, hd), f32),
            pltpu.VMEM((n_sub, c_rows, hd), f32),
            pltpu.VMEM((n_heads, n_sel_pad), f32),
            pltpu.VMEM((n_heads, 1), f32),
            pltpu.VMEM((n_heads, 1), f32),
            pltpu.VMEM((n_heads, hd), f32),
            pltpu.VMEM((n_heads, hd), f32),
        ],
    )
    return pl.pallas_call(
        kern,
        out_shape=jax.ShapeDtypeStruct((batch, n_heads, hd), f32),
        grid_spec=grid_spec,
        compiler_params=_cparams(("arbitrary", "arbitrary", "arbitrary")),
        name="nsa_decode",
    )(page_table, *([pool] * pps), q, kvn, wn, gate, cwin, wh, wt)


def _group_lanes(v, groups):
    e = v.shape[-1] // groups
    v = v.reshape(v.shape[:-1] + (groups, e))
    v = jnp.pad(v, [(0, 0)] * (v.ndim - 1) + [(0, V7X_LANES - e)])
    return v.reshape(v.shape[:-2] + (groups * V7X_LANES,))


def _prep_dense(dims, ffn_w_up, ffn_w_down, ple_w_gate, ple_w_proj, ab_w_in, ab_w_out, nsa_w_in, nsa_w_out):
    return {
        "up": ffn_w_up.astype(bf16),
        "down": ffn_w_down.astype(bf16),
        "ple_gate": ple_w_gate.astype(bf16),
        "ple_proj": ple_w_proj.astype(bf16),
        "ab_in": ab_w_in.astype(bf16),
        "ab_out": ab_w_out.astype(bf16),
        "nsa_in": nsa_w_in.astype(bf16),
        "nsa_out": nsa_w_out.astype(bf16),
    }


def _prep_ab(j, dims, ab_w_in, ssm_dt_bias, ssm_a_log, ssm_d):
    groups = dims["groups"]
    o3 = 2 * dims["conv_ch"] + dims["d_inner"] + dims["xbc"]
    return {
        "dt": _group_lanes(ab_w_in[j, :, o3:], groups).astype(bf16),
        "dt_bias": _group_lanes(ssm_dt_bias[j].astype(f32)[None], groups),
        "a": _group_lanes(-jnp.exp(ssm_a_log[j].astype(f32))[None], groups).reshape(groups, 1, V7X_LANES),
        "d": _group_lanes(ssm_d[j].astype(f32)[None], groups).reshape(groups, 1, V7X_LANES),
    }


def _prep_nsa(j, dims, nsa_w_in, nsa_cmp_w):
    qd, kvh, hd, n_heads = dims["qd"], dims["kvh"], dims["hd"], dims["n_heads"]
    e_heads = n_heads // kvh
    wg = nsa_w_in[j, :, qd + 6 * kvh * hd:]
    wg_grp = wg.reshape(-1, 3, kvh, e_heads).transpose(0, 2, 1, 3).reshape(-1, kvh * 3 * e_heads)
    cw = nsa_cmp_w[j].astype(f32)
    cmp_len = cw.shape[1]
    cw = cw.reshape(2, cmp_len, kvh * hd)
    half = cmp_len // 2
    return {
        "gate_grp": _group_lanes(wg_grp, kvh).astype(bf16),
        "gate_flat": jnp.pad(wg, ((0, 0), (0, V7X_LANES - wg.shape[1] % V7X_LANES))).astype(bf16),
        "wh": jnp.concatenate([cw[0, :half], cw[1, :half]], axis=1),
        "wt": jnp.concatenate([cw[0, half:], cw[1, half:]], axis=1),
        "cmp_len": cmp_len,
    }


_DOWN_BK_CAP = 5504


def _ffn(x, g, dense, i, s, dims):
    h = _rmsnorm(x, g)
    ffn = dims["ffn"]
    act = _matmul([h], [_W(dense["up"], (i, s)), _W(dense["up"], (i, s), col_off=ffn)], [(0, 0, 0), (0, 1, 1)], 2,
                  _ep_swiglu, ffn, out_dtype=bf16, bn_cap=512, bm_cap=2048)
    units = ffn // V7X_LANES
    nk = min(d for d in range(1, units + 1) if units % d == 0 and ffn // d <= _DOWN_BK_CAP)
    return _matmul([act], [_W(dense["down"], (i, s))], [(0, 0, 0)], 1, _ep_half_resid, x.shape[1],
                   extras=[(x, "mn")], nk=nk)


def _pad_rows(a, batch, rows):
    c = a.shape[1]
    return jnp.pad(a[:, None, :], ((0, 0), (0, rows - 1), (0, 0))).reshape(batch * rows, c)


def _ab_mixer(x, h, dense, w, prm, j, st, dims, batch):
    m = x.shape[0]
    t = m // batch
    ch, d_inner, groups = dims["conv_ch"], dims["d_inner"], dims["groups"]
    w_in = dense["ab_in"]
    u = _matmul([h], [_W(w_in, (j,)), _W(w_in, (j,), col_off=ch)], [(0, 0, 0), (0, 1, 1)], 2, _ep_glu, ch)
    z = _matmul([h], [_W(w_in, (j,), col_off=2 * ch)], [(0, 0, 0)], 1, _ep_plain, d_inner)
    xbc = _matmul([h], [_W(w_in, (j,), col_off=2 * ch + d_inner)], [(0, 0, 0)], 1, _ep_plain, dims["xbc"])
    dtg = _matmul([h], [_W(w["dt"])], [(0, 0, 0)], 1, _ep_softplus_bias, groups * V7X_LANES,
                  extras=[(w["dt_bias"], "n")])
    conv_state, sconv_state, ssm_state = st
    if t == 1:
        rc = V7X_SUBLANES
        c_pre = _causal_conv(_pad_rows(u, batch, rc), conv_state, prm["conv_dw_w"][j], prm["conv_dw_b"][j],
                             batch=batch, silu=False)[::rc]
        xbc_c = _causal_conv(_pad_rows(xbc, batch, rc), sconv_state, prm["ssm_conv_w"][j], prm["ssm_conv_b"][j],
                             batch=batch, silu=True)[::rc]
        rq = SSM_CHUNK
        y, h_new = _ssd(_pad_rows(xbc_c, batch, rq), _pad_rows(dtg, batch, rq), _pad_rows(z, batch, rq),
                        w["a"], w["d"], prm["ssm_norm_g"][j], ssm_state.astype(f32), batch=batch, groups=groups)
        y = y[::rq]
    else:
        c_pre = _causal_conv(u, conv_state, prm["conv_dw_w"][j], prm["conv_dw_b"][j], batch=batch, silu=False)
        xbc_c = _causal_conv(xbc, sconv_state, prm["ssm_conv_w"][j], prm["ssm_conv_b"][j], batch=batch, silu=True)
        y, h_new = _ssd(xbc_c, dtg, z, w["a"], w["d"], prm["ssm_norm_g"][j], ssm_state.astype(f32), batch=batch,
                        groups=groups)
    c = _ln_silu(c_pre, prm["conv_ln_g"][j], prm["conv_ln_b"][j])
    assert d_inner % ch == 0
    n_y = d_inner // ch
    xs = [c] + [_X(y, ch, r * ch) for r in range(n_y)]
    ws = [_W(dense["ab_out"], (j,), rows=ch, row_off=r * ch) for r in range(n_y + 1)]
    x = _matmul(xs, ws, [(r, r, 0) for r in range(n_y + 1)], 1, _ep_resid, x.shape[1], extras=[(x, "mn")])
    k1 = conv_state.shape[1]
    k2 = sconv_state.shape[1]
    u_ext = jnp.concatenate([conv_state.astype(f32), u.reshape(batch, t, -1)], axis=1)[:, -k1:]
    xbc_ext = jnp.concatenate([sconv_state.astype(f32), xbc.reshape(batch, t, -1)], axis=1)[:, -k2:]
    return x, u_ext, xbc_ext, h_new


def _nsa_mixer(x, h, dense, w, j, past, dims, batch):
    m = x.shape[0]
    t = m // batch
    qd, kvh, hd, n_heads, window = dims["qd"], dims["kvh"], dims["hd"], dims["n_heads"], dims["window"]
    w_in = dense["nsa_in"]
    kv = _matmul([h], [_W(w_in, (j,), col_off=qd)], [(0, 0, 0)], 1, _ep_plain, 4 * kvh * hd)
    win = _matmul([h], [_W(w_in, (j,), col_off=qd + 4 * kvh * hd)], [(0, 0, 0)], 1, _ep_plain, 2 * kvh * hd)
    if past is None:
        q = _matmul([h], [_W(w_in, (j,))], [(0, 0, 0)], 1, _ep_plain, qd, out_dtype=bf16)
        gate = _matmul([h], [_W(w["gate_grp"])], [(0, 0, 0)], 1, _ep_sigmoid, kvh * V7X_LANES)
        cmp = _compress(kv, w["wh"], w["wt"], batch=batch)
        o = _nsa_prompt(q, kv.astype(bf16), win.astype(bf16), cmp, gate, batch=batch, kvh=kvh, hd=hd,
                        window=window, cmp_len=w["cmp_len"])
        new_win = win.reshape(batch, t, 2, kvh, hd)[:, -window:]
    else:
        assert t == 1
        cache_kv, cache_win, page_table = past
        q = _matmul([h], [_W(w_in, (j,))], [(0, 0, 0)], 1, _ep_plain, qd)
        gate = _matmul([h], [_W(w["gate_flat"])], [(0, 0, 0)], 1, _ep_sigmoid, w["gate_flat"].shape[1])
        gate = gate[:, :3 * n_heads].reshape(batch, 3, n_heads).transpose(0, 2, 1)
        gate = jnp.pad(gate, ((0, 0), (0, 0), (0, V7X_LANES - 3)))
        pool = cache_kv.reshape(cache_kv.shape[:3] + (4 * kvh, hd))
        cwin = cache_win.reshape(cache_win.shape[:3] + (2 * kvh, hd))
        o = _nsa_decode(q.reshape(batch, n_heads, hd), kv.reshape(batch, 4 * kvh, hd), win.reshape(batch, 2 * kvh, hd),
                        gate, pool, cwin, page_table, w["wh"].reshape(-1, 2 * kvh, hd), w["wt"].reshape(-1, 2 * kvh, hd),
                        layer=j, kvh=kvh, hd=hd, window=window, cmp_len=w["cmp_len"])
        o = o.reshape(batch, qd).astype(bf16)
        new_win = jnp.concatenate([cache_win[j].astype(f32), win.reshape(batch, t, 2, kvh, hd)], axis=1)
        new_win = new_win[:, -min(window, new_win.shape[1]):]
    x = _matmul([o], [_W(dense["nsa_out"], (j,))], [(0, 0, 0)], 1, _ep_resid, x.shape[1], extras=[(x, "mn")])
    return x, kv.reshape(batch, t, 4, kvh, hd), new_win


def _trunk(x, p, prm, dense, ab, nsa, states, past, dims, batch):
    depth = prm["norm_g"].shape[0]
    kv_rows, win_new, conv_new, sconv_new, ssm_new = [], [], [], [], []
    for i in range(depth):
        j = i // 2
        g = prm["norm_g"][i]
        x = _ffn(x, g[0], dense, i, 0, dims)
        h = _rmsnorm(x, g[1])
        if i % 2 == 0:
            x, cb, scb, sh = _ab_mixer(x, h, dense, ab[j], prm, j, states[j], dims, batch)
            conv_new.append(cb)
            sconv_new.append(scb)
            ssm_new.append(sh)
        else:
            x, rows, wb = _nsa_mixer(x, h, dense, nsa[j], j, past, dims, batch)
            kv_rows.append(rows)
            win_new.append(wb)
        x = _ffn(x, g[2], dense, i, 1, dims)
        hn = _rmsnorm(x, g[3])
        x = _matmul([hn, p[i].astype(bf16)], [_W(dense["ple_gate"], (i,)), _W(dense["ple_proj"], (i,))],
                    [(0, 0, 0), (1, 1, 1)], 2, _ep_ple, x.shape[1], extras=[(x, "mn")])
    y = _rmsnorm(x, prm["final_norm_g"], out_dtype=f32)
    return (y, jnp.stack(kv_rows), jnp.stack(win_new), jnp.stack(conv_new), jnp.stack(sconv_new),
            jnp.stack(ssm_new))


def kernel(x_prompt, x_sample, cache_nsa_kv, cache_nsa_win, state_conv, state_ssm_conv, state_ssm, page_table,
           p_prompt, p_sample, norm_g, final_norm_g, ffn_w_up, ffn_w_down, ple_w_gate, ple_w_proj, ab_w_in,
           ab_w_out, conv_dw_w, conv_dw_b, conv_ln_g, conv_ln_b, ssm_conv_w, ssm_conv_b, ssm_dt_bias, ssm_a_log,
           ssm_d, ssm_norm_g, nsa_w_in, nsa_w_out, nsa_cmp_w):
    bp, seq, d_model = x_prompt.shape
    bs, dec_seq, _ = x_sample.shape
    depth = norm_g.shape[0]
    n_ab, _, heads, p_dim, n_state = state_ssm.shape
    n_c = cache_nsa_kv.shape[0]
    kvh, hd = cache_nsa_kv.shape[-2:]
    d_inner = heads * p_dim
    xbc = state_ssm_conv.shape[-1]
    ffn = ffn_w_down.shape[2]
    dims = {
        "ffn": ffn,
        "conv_ch": state_conv.shape[-1], "d_inner": d_inner, "xbc": xbc,
        "groups": (xbc - d_inner) // (2 * n_state),
        "qd": nsa_w_out.shape[1], "kvh": kvh, "hd": hd, "n_heads": nsa_w_out.shape[1] // hd,
        "window": cache_nsa_win.shape[2],
    }
    prm = {
        "norm_g": norm_g, "final_norm_g": final_norm_g, "conv_dw_w": conv_dw_w, "conv_dw_b": conv_dw_b,
        "conv_ln_g": conv_ln_g, "conv_ln_b": conv_ln_b, "ssm_conv_w": ssm_conv_w, "ssm_conv_b": ssm_conv_b,
        "ssm_norm_g": ssm_norm_g,
    }
    dense = _prep_dense(dims, ffn_w_up, ffn_w_down, ple_w_gate, ple_w_proj, ab_w_in, ab_w_out, nsa_w_in, nsa_w_out)
    ab = [_prep_ab(j, dims, ab_w_in, ssm_dt_bias, ssm_a_log, ssm_d) for j in range(n_ab)]
    nsa = [_prep_nsa(j, dims, nsa_w_in, nsa_cmp_w) for j in range(n_c)]

    zero_states = [
        (jnp.zeros((bp,) + state_conv.shape[2:], f32), jnp.zeros((bp,) + state_ssm_conv.shape[2:], f32),
         jnp.zeros((bp, heads, p_dim, n_state), f32))
        for _ in range(n_ab)
    ]
    y_p, kv_p, win_p, conv_p, sconv_p, ssm_p = _trunk(
        x_prompt.reshape(bp * seq, d_model), p_prompt.reshape(depth, bp * seq, -1), prm, dense, ab, nsa,
        zero_states, None, dims, bp)

    assert dec_seq == 1
    states = [(state_conv[j], state_ssm_conv[j], state_ssm[j]) for j in range(n_ab)]
    y_s, kv_s, win_s, conv_s, sconv_s, ssm_s = _trunk(
        x_sample.reshape(bs * dec_seq, d_model), p_sample.reshape(depth, bs * dec_seq, -1), prm, dense, ab, nsa,
        states, (cache_nsa_kv, cache_nsa_win, page_table), dims, bs)

    return (y_p.reshape(bp, seq, d_model), y_s.reshape(bs, dec_seq, d_model), kv_p, kv_s, win_p, win_s,
            conv_p, conv_s, sconv_p, sconv_s, ssm_p, ssm_s)
```
